```python
import jax, jax.numpy as jnp
from jax import lax
import numpy as np

D_MODEL = 1024
BATCH = 16
SEQ = 4096
DEPTH = 1

POOL_WIDTH = D_MODEL // 2
POOL_GROUPS = 4
POOL_GROUP_DIM = POOL_WIDTH // POOL_GROUPS
POOL_WINDOWS = (2, 4, 8, 16)
LRU_WIDTH = D_MODEL
LRU_HEADS = 16
LRU_HEAD_DIM = LRU_WIDTH // LRU_HEADS
CONV_WIDTH = 4
RG_C = 8.0
IN_COLS = POOL_WIDTH + 2 * LRU_WIDTH + 2 * D_MODEL
N_GROUPS = 4
EXP_PER_GROUP = 4
N_EXP = N_GROUPS * EXP_PER_GROUP
TOP_K = 2
D_EXPERT = D_MODEL // 2
EPS = 1e-6

kernel_name = 'hybrid_pool_rglru_hmoe_block'


def rms_norm(x, g):
    xf = x.astype(jnp.float32)
    y = xf * lax.rsqrt(jnp.mean(xf * xf, axis=-1, keepdims=True) + EPS)
    return (y * g.astype(jnp.float32)).astype(x.dtype)


def pooling_mixer(u, pool_w, pool_scale):
    b_, s_, _ = u.shape
    ug = u.astype(jnp.float32).reshape(b_, s_, POOL_GROUPS, POOL_GROUP_DIM)
    c = jnp.cumsum(ug, axis=1)
    pos = jnp.arange(1, s_ + 1, dtype=jnp.float32)
    pooled = []
    for g, w in enumerate(POOL_WINDOWS):
        cg = c[:, :, g]
        prev = jnp.pad(cg[:, :s_ - w], ((0, 0), (w, 0), (0, 0)))
        cnt = jnp.minimum(pos, float(w))
        pooled.append((cg - prev) / cnt[None, :, None])
    z = (jnp.stack(pooled, axis=2) - ug).astype(u.dtype)
    y = jnp.einsum('bsgc,gcd->bsgd', z, pool_w).reshape(b_, s_, POOL_WIDTH)
    return y * pool_scale


def rglru_mixer(u, gate_in, conv_w, conv_b, w_r, b_r, w_i, b_i, lam):
    b_, s_, c_ = u.shape
    xc = lax.conv_general_dilated(
        u, conv_w[:, None, :], window_strides=(1,), padding=[(CONV_WIDTH - 1, 0)],
        dimension_numbers=('NWC', 'WIO', 'NWC'), feature_group_count=c_) + conv_b
    xh = xc.reshape(b_, s_, LRU_HEADS, LRU_HEAD_DIM)
    r = jax.nn.sigmoid(jnp.einsum('bshc,hcd->bshd', xh, w_r).reshape(b_, s_, c_) + b_r)
    i = jax.nn.sigmoid(jnp.einsum('bshc,hcd->bshd', xh, w_i).reshape(b_, s_, c_) + b_i)
    log_a = -RG_C * r.astype(jnp.float32) * jax.nn.softplus(-lam.astype(jnp.float32))
    a = jnp.exp(log_a)
    mult = jnp.sqrt(-jnp.expm1(2.0 * log_a))
    bt = mult * (i * xc).astype(jnp.float32)

    def combine(lhs, rhs):
        a1, b1 = lhs
        a2, b2 = rhs
        return a1 * a2, a2 * b1 + b2

    _, h = lax.associative_scan(combine, (a, bt), axis=1)
    return h.astype(u.dtype) * jax.nn.gelu(gate_in)


def hier_moe(h, rg_w, rg_b, re_w, re_b, e_gate, e_up, e_down):
    b_, s_, d_ = h.shape
    xt = h.reshape(b_ * s_, d_)
    g_logits = (xt @ rg_w + rg_b).astype(jnp.float32)
    g_prob = jax.nn.softmax(g_logits, axis=-1)
    p_top, g_idx = lax.top_k(g_prob, 1)
    e_logits = (xt @ re_w + re_b).astype(jnp.float32).reshape(-1, N_GROUPS, EXP_PER_GROUP)
    sel = jnp.take_along_axis(e_logits, g_idx[:, :, None], axis=1)[:, 0]
    v, e_idx = lax.top_k(sel, TOP_K)
    w2 = jax.nn.softmax(v, axis=-1) * p_top
    ids = g_idx * EXP_PER_GROUP + e_idx
    comb = jnp.einsum('tk,tke->te', w2, jax.nn.one_hot(ids, N_EXP, dtype=jnp.float32)).astype(h.dtype)
    y = jnp.zeros_like(xt)
    for e in range(N_EXP):
        he = jax.nn.silu(xt @ e_gate[e]) * (xt @ e_up[e])
        y = y + comb[:, e:e + 1] * (he @ e_down[e])
    return y.reshape(b_, s_, d_)


def setup_inputs(seed: int = 0) -> dict:
    key = jax.random.key(seed)
    ks = jax.random.split(key, 24)
    f32 = jnp.float32
    L = DEPTH

    def nrm(k, shape, scale):
        return jax.random.normal(k, shape, f32) * scale

    u = jax.random.uniform(ks[10], (L, LRU_WIDTH), f32, 0.9, 0.999)
    s = u ** (1.0 / RG_C)
    lam = jnp.log(s) - jnp.log1p(-s)
    return {
        'x': nrm(ks[0], (BATCH, SEQ, D_MODEL), 1.0),
        'norm1_g': 1.0 + nrm(ks[1], (L, D_MODEL), 0.02),
        'w_in': nrm(ks[2], (L, D_MODEL, IN_COLS), D_MODEL ** -0.5),
        'pool_w': nrm(ks[3], (L, POOL_GROUPS, POOL_GROUP_DIM, POOL_GROUP_DIM), POOL_GROUP_DIM ** -0.5),
        'pool_scale': 1.0 + nrm(ks[4], (L, POOL_WIDTH), 0.02),
        'conv_w': nrm(ks[5], (L, CONV_WIDTH, LRU_WIDTH), CONV_WIDTH ** -0.5),
        'conv_b': nrm(ks[6], (L, LRU_WIDTH), 0.01),
        'rg_w_r': nrm(ks[7], (L, LRU_HEADS, LRU_HEAD_DIM, LRU_HEAD_DIM), LRU_HEAD_DIM ** -0.5),
        'rg_b_r': nrm(ks[8], (L, LRU_WIDTH), 0.01),
        'rg_w_i': nrm(ks[9], (L, LRU_HEADS, LRU_HEAD_DIM, LRU_HEAD_DIM), LRU_HEAD_DIM ** -0.5),
        'rg_b_i': nrm(ks[11], (L, LRU_WIDTH), 0.01),
        'rg_lambda': lam,
        'proj_a': nrm(ks[12], (L, POOL_WIDTH, D_MODEL), POOL_WIDTH ** -0.5),
        'proj_b': nrm(ks[13], (L, LRU_WIDTH, D_MODEL), LRU_WIDTH ** -0.5),
        'w_out': nrm(ks[14], (L, D_MODEL, D_MODEL), D_MODEL ** -0.5),
        'norm2_g': 1.0 + nrm(ks[15], (L, D_MODEL), 0.02),
        'router_group_w': nrm(ks[16], (L, D_MODEL, N_GROUPS), D_MODEL ** -0.5),
        'router_group_b': nrm(ks[17], (L, N_GROUPS), 0.01),
        'router_expert_w': nrm(ks[18], (L, D_MODEL, N_EXP), D_MODEL ** -0.5),
        'router_expert_b': nrm(ks[19], (L, N_EXP), 0.01),
        'exp_w_gate': nrm(ks[20], (L, N_EXP, D_MODEL, D_EXPERT), D_MODEL ** -0.5),
        'exp_w_up': nrm(ks[21], (L, N_EXP, D_MODEL, D_EXPERT), D_MODEL ** -0.5),
        'exp_w_down': nrm(ks[22], (L, N_EXP, D_EXPERT, D_MODEL), D_EXPERT ** -0.5),
        'norm_f_g': 1.0 + nrm(ks[23], (D_MODEL,), 0.02),
    }


def reference(x, norm1_g, w_in, pool_w, pool_scale, conv_w, conv_b, rg_w_r, rg_b_r, rg_w_i, rg_b_i,
              rg_lambda, proj_a, proj_b, w_out, norm2_g, router_group_w, router_group_b,
              router_expert_w, router_expert_b, exp_w_gate, exp_w_up, exp_w_down, norm_f_g):
    o1 = POOL_WIDTH
    o2 = o1 + LRU_WIDTH
    o3 = o2 + LRU_WIDTH
    o4 = o3 + D_MODEL
    for l in range(DEPTH):
        h = rms_norm(x, norm1_g[l])
        z = h @ w_in[l]
        u_pool, u_lru, u_gate = z[..., :o1], z[..., o1:o2], z[..., o2:o3]
        g_a = jax.nn.sigmoid(z[..., o3:o4])
        g_b = jax.nn.sigmoid(z[..., o4:])
        y_a = pooling_mixer(u_pool, pool_w[l], pool_scale[l]) @ proj_a[l]
        y_b = rglru_mixer(u_lru, u_gate, conv_w[l], conv_b[l], rg_w_r[l], rg_b_r[l],
                          rg_w_i[l], rg_b_i[l], rg_lambda[l]) @ proj_b[l]
        x = x + (g_a * y_a + g_b * y_b) @ w_out[l]
        h2 = rms_norm(x, norm2_g[l])
        x = x + hier_moe(h2, router_group_w[l], router_group_b[l], router_expert_w[l],
                         router_expert_b[l], exp_w_gate[l], exp_w_up[l], exp_w_down[l])
    return rms_norm(x, norm_f_g)
```

```python
import functools

import jax
import jax.numpy as jnp
from jax import lax
from jax.experimental import pallas as pl
from jax.experimental.pallas import tpu as pltpu

F32 = jnp.float32
BF16 = jnp.bfloat16

D_MODEL = 1024
POOL_WIDTH = 512
POOL_GROUP_DIM = 128
POOL_WINDOWS = (2, 4, 8, 16)
POOL_HALO = 16
LRU_WIDTH = 1024
CONV_WIDTH = 4
CONV_HALO = 8
RG_C = 8.0
N_GROUPS = 4
EXP_PER_GROUP = 4
N_EXP = 16
D_EXPERT = 512
EPS = 1e-6
N_PAIRS = 6
N_CLASSES = N_GROUPS * N_PAIRS

LANES = 128
SUBLANES = 8
MXU_DIM = 256
META_W = LANES
ROW_W = D_MODEL + META_W

O_POOL = 0
O_LRU = POOL_WIDTH
O_GATE = O_LRU + LRU_WIDTH
O_GA = O_GATE + LRU_WIDTH
O_GB = O_GA + D_MODEL
IN_COLS = O_GB + D_MODEL

SEQ_TILE = 256
TOKEN_TILE = 512
EXPERT_TILE = 512
VMEM_LIMIT = 56 * 1024 * 1024


def _sigmoid(v):
    return 0.5 * jnp.tanh(0.5 * v) + 0.5


def _gelu_tanh(v):
    c = 0.7978845608028654
    return 0.5 * v * (1.0 + jnp.tanh(c * (v + 0.044715 * (v * v * v))))


def _rms_norm(v, g):
    ms = jnp.mean(v * v, axis=-1, keepdims=True)
    return v * lax.rsqrt(ms + EPS) * g


def _dot(a, b):
    return jnp.dot(a, b, preferred_element_type=F32)


def _mixer_kernel(x_ref, g1_ref, w_in_ref, pool_bd_ref, pool_scale_ref, conv_w_ref, conv_b_ref,
                  wr_ref, wi_ref, br_ref, bi_ref, lam_ref, pa_ref, pb_ref, wo_ref, g2_ref,
                  wrt_ref, brt_ref,
                  x1_ref, h2e_ref, cnt_ref,
                  pool_ext, lru_ext, a_buf, b_buf, h_carry, cnt_scr):
    b = pl.program_id(0)
    c = pl.program_id(1)
    ts = x_ref.shape[1]

    @pl.when(c == 0)
    def _():
        pool_ext[0:POOL_HALO, :] = jnp.zeros((POOL_HALO, POOL_WIDTH), F32)
        lru_ext[0:CONV_HALO, :] = jnp.zeros((CONV_HALO, LRU_WIDTH), F32)
        h_carry[...] = jnp.zeros_like(h_carry)

    @pl.when((b == 0) & (c == 0))
    def _():
        cnt_scr[...] = jnp.zeros_like(cnt_scr)

    x = x_ref[0]
    h = _rms_norm(x, g1_ref[...]).astype(BF16)

    def proj(lo, width):
        return _dot(h, w_in_ref[:, lo:lo + width])

    pool_ext[POOL_HALO:POOL_HALO + ts, :] = proj(O_POOL, POOL_WIDTH)
    ext = pool_ext[...]
    pool_ext[0:POOL_HALO, :] = ext[ts:ts + POOL_HALO, :]
    pos1 = (c * ts + 1 + lax.broadcasted_iota(jnp.int32, (ts, POOL_GROUP_DIM), 0)).astype(F32)
    zs = []
    for g, w in enumerate(POOL_WINDOWS):
        s = ext[:, g * POOL_GROUP_DIM:(g + 1) * POOL_GROUP_DIM]
        u = s[POOL_HALO:]
        d = 1
        while d < w:
            s = s[d:] + s[:-d]
            d *= 2
        off = POOL_HALO - (w - 1)
        win = s[off:off + ts]
        zs.append(win / jnp.minimum(pos1, float(w)) - u)
    z = jnp.concatenate(zs, axis=1).astype(BF16)
    y_pool = jnp.concatenate(
        [_dot(z[:, j * MXU_DIM:(j + 1) * MXU_DIM], pool_bd_ref[j]) for j in range(POOL_WIDTH // MXU_DIM)],
        axis=1) * pool_scale_ref[...]
    y_a = _dot(y_pool.astype(BF16), pa_ref[...])
    mix = _sigmoid(proj(O_GA, D_MODEL)) * y_a

    lru_ext[CONV_HALO:CONV_HALO + ts, :] = proj(O_LRU, LRU_WIDTH)
    e = lru_ext[...]
    lru_ext[0:CONV_HALO, :] = e[ts:ts + CONV_HALO, :]
    xc = conv_b_ref[...] + conv_w_ref[CONV_WIDTH - 1:CONV_WIDTH, :] * e[CONV_HALO:]
    for k in range(CONV_WIDTH - 1):
        o = CONV_HALO - (CONV_WIDTH - 1) + k
        xc = xc + conv_w_ref[k:k + 1, :] * e[o:o + ts]
    xcb = xc.astype(BF16)
    nblk = LRU_WIDTH // MXU_DIM
    r = _sigmoid(jnp.concatenate(
        [_dot(xcb[:, j * MXU_DIM:(j + 1) * MXU_DIM], wr_ref[j]) for j in range(nblk)], axis=1) + br_ref[...])
    i = _sigmoid(jnp.concatenate(
        [_dot(xcb[:, j * MXU_DIM:(j + 1) * MXU_DIM], wi_ref[j]) for j in range(nblk)], axis=1) + bi_ref[...])
    nlam = -lam_ref[...]
    softplus = jnp.maximum(nlam, 0.0) + jnp.log1p(jnp.exp(-jnp.abs(nlam)))
    log_a = (-RG_C * softplus) * r
    a = jnp.exp(log_a)
    a_buf[...] = a
    b_buf[...] = jnp.sqrt(jnp.tanh(-log_a) * (1.0 + a * a)) * (i * xc)

    row = lax.broadcasted_iota(jnp.int32, (SUBLANES, LRU_WIDTH), 0)

    def scan_body(gidx, carry):
        r0 = pl.multiple_of(gidx * SUBLANES, SUBLANES)
        av = a_buf[pl.ds(r0, SUBLANES), :]
        bv = b_buf[pl.ds(r0, SUBLANES), :]
        for d in (1, 2, 4):
            a_sh = jnp.where(row >= d, pltpu.roll(av, d, 0), 1.0)
            b_sh = jnp.where(row >= d, pltpu.roll(bv, d, 0), 0.0)
            bv = bv + av * b_sh
            av = av * a_sh
        hv = bv + av * carry
        b_buf[pl.ds(r0, SUBLANES), :] = hv
        return hv[SUBLANES - 1:SUBLANES, :]

    h_carry[0:1, :] = lax.fori_loop(0, ts // SUBLANES, scan_body, h_carry[0:1, :])
    y_lru = b_buf[...] * _gelu_tanh(proj(O_GATE, LRU_WIDTH))
    y_b = _dot(y_lru.astype(BF16), pb_ref[...])
    mix = mix + _sigmoid(proj(O_GB, D_MODEL)) * y_b

    x1 = x + _dot(mix.astype(BF16), wo_ref[...])
    x1_ref[0] = x1
    h2 = _rms_norm(x1, g2_ref[...])
    h2e_ref[:, 0:D_MODEL] = h2

    logits = _dot(h2.astype(BF16), wrt_ref[...]) + brt_ref[...]
    lane = lax.broadcasted_iota(jnp.int32, (ts, LANES), 1).astype(F32)
    neg = jnp.float32(-3.0e38)
    big = jnp.float32(LANES)
    gl = jnp.where(lane < N_GROUPS, logits, neg)
    gmax = jnp.max(gl, axis=-1, keepdims=True)
    gsum = jnp.sum(jnp.where(lane < N_GROUPS, jnp.exp(gl - gmax), 0.0), axis=-1, keepdims=True)
    p_top = 1.0 / gsum
    gidx = jnp.min(jnp.where(gl == gmax, lane, big), axis=-1, keepdims=True)
    e0 = N_GROUPS + EXP_PER_GROUP * gidx
    el = jnp.where((lane >= e0) & (lane < e0 + EXP_PER_GROUP), logits, neg)
    v1 = jnp.max(el, axis=-1, keepdims=True)
    i1 = jnp.min(jnp.where(el == v1, lane, big), axis=-1, keepdims=True)
    el2 = jnp.where(lane == i1, neg, el)
    v2 = jnp.max(el2, axis=-1, keepdims=True)
    i2 = jnp.min(jnp.where(el2 == v2, lane, big), axis=-1, keepdims=True)
    ex = jnp.exp(v2 - v1)
    w1 = p_top / (1.0 + ex)
    w2 = p_top * ex / (1.0 + ex)
    first_lo = i1 < i2
    w_lo = jnp.where(first_lo, w1, w2)
    w_hi = jnp.where(first_lo, w2, w1)
    pa_ = jnp.minimum(i1, i2) - e0
    pb_ = jnp.maximum(i1, i2) - e0
    cls = gidx * N_PAIRS + pa_ * (7.0 - pa_) * 0.5 + (pb_ - pa_ - 1.0)

    onehot = lane == cls
    rr = lax.broadcasted_iota(jnp.int32, (ts, ts), 0)
    cc = lax.broadcasted_iota(jnp.int32, (ts, ts), 1)
    ltri = jnp.where(rr > cc, 1.0, 0.0).astype(BF16)
    prior = _dot(ltri, jnp.where(onehot, 1.0, 0.0).astype(BF16)) + cnt_scr[0:1, :]
    rank = jnp.sum(jnp.where(onehot, prior, 0.0), axis=-1, keepdims=True)
    cnt_scr[0:1, :] = cnt_scr[0:1, :] + jnp.sum(jnp.where(onehot, 1.0, 0.0), axis=0, keepdims=True)
    cnt_ref[...] = cnt_scr[...]

    meta = jnp.where(lane == 0.0, w_lo,
                     jnp.where(lane == 1.0, w_hi,
                               jnp.where(lane == 2.0, cls,
                                         jnp.where(lane == 3.0, rank, 0.0))))
    h2e_ref[:, D_MODEL:ROW_W] = meta


def _const_spec(shape):
    zeros = (0,) * len(shape)
    return pl.BlockSpec(shape, lambda b, c: zeros, pipeline_mode=pl.Buffered(1))


def _mixer(x, g1, w_in, pool_bd, pool_scale, conv_w, conv_b, wr4, wi4, b_r, b_i, lam, pa, pb, wo, g2,
           wrt, brt):
    bsz, seq, d = x.shape
    ts = min(SEQ_TILE, seq)
    assert seq % ts == 0 and ts % SUBLANES == 0 and ts >= POOL_HALO and d == D_MODEL
    nc = seq // ts
    consts = (g1, w_in, pool_bd, pool_scale, conv_w, conv_b, wr4, wi4, b_r, b_i, lam, pa, pb, wo, g2,
              wrt, brt)
    return pl.pallas_call(
        _mixer_kernel,
        grid=(bsz, nc),
        in_specs=[pl.BlockSpec((1, ts, d), lambda b, c: (b, c, 0))] + [_const_spec(a.shape) for a in consts],
        out_specs=[
            pl.BlockSpec((1, ts, d), lambda b, c: (b, c, 0)),
            pl.BlockSpec((ts, ROW_W), lambda b, c: (b * nc + c, 0)),
            pl.BlockSpec((SUBLANES, LANES), lambda b, c: (0, 0)),
        ],
        out_shape=[
            jax.ShapeDtypeStruct((bsz, seq, d), F32),
            jax.ShapeDtypeStruct((bsz * seq, ROW_W), F32),
            jax.ShapeDtypeStruct((SUBLANES, LANES), F32),
        ],
        scratch_shapes=[
            pltpu.VMEM((POOL_HALO + ts, POOL_WIDTH), F32),
            pltpu.VMEM((CONV_HALO + ts, LRU_WIDTH), F32),
            pltpu.VMEM((ts, LRU_WIDTH), F32),
            pltpu.VMEM((ts, LRU_WIDTH), F32),
            pltpu.VMEM((SUBLANES, LRU_WIDTH), F32),
            pltpu.VMEM((SUBLANES, LANES), F32),
        ],
        compiler_params=pltpu.CompilerParams(
            dimension_semantics=("arbitrary", "arbitrary"), vmem_limit_bytes=VMEM_LIMIT),
        name="mixer",
    )(x, *consts)


def _dispatch_kernel(pos_ref, h2e_ref, xs_in_ref, xs_ref, sem):
    del xs_in_ref
    tb = h2e_ref.shape[0]
    base = pl.program_id(0) * tb

    def row_copy(r, p):
        return pltpu.make_async_copy(h2e_ref.at[pl.ds(r, 1)], xs_ref.at[pl.ds(p, 1)], sem)

    def issue(r, carry):
        row_copy(r, pos_ref[base + r]).start()
        return carry

    lax.fori_loop(0, tb, issue, 0)

    def drain(r, carry):
        row_copy(r, 0).wait()
        return carry

    lax.fori_loop(0, tb, drain, 0)


def _dispatch(pos, h2e, n_slots):
    t = h2e.shape[0]
    tb = min(TOKEN_TILE, t)
    assert t % tb == 0
    xs0 = jnp.zeros((n_slots, ROW_W), F32)
    return pl.pallas_call(
        _dispatch_kernel,
        grid_spec=pltpu.PrefetchScalarGridSpec(
            num_scalar_prefetch=1,
            grid=(t // tb,),
            in_specs=[pl.BlockSpec((tb, ROW_W), lambda i, pos: (i, 0)),
                      pl.BlockSpec(memory_space=pl.ANY)],
            out_specs=pl.BlockSpec(memory_space=pl.ANY),
            scratch_shapes=[pltpu.SemaphoreType.DMA(())],
        ),
        out_shape=jax.ShapeDtypeStruct((n_slots, ROW_W), F32),
        input_output_aliases={2: 0},
        compiler_params=pltpu.CompilerParams(
            dimension_semantics=("arbitrary",), vmem_limit_bytes=VMEM_LIMIT),
        name="dispatch",
    )(pos, h2e, xs0)


def _expert_kernel(meta_ref, xs_ref, wg_lo, wu_lo, wd_lo, wg_hi, wu_hi, wd_hi, ys_ref):
    i = pl.program_id(0)

    @pl.when(i < meta_ref[0])
    def _():
        x = xs_ref[:, 0:D_MODEL].astype(BF16)
        w_lo = xs_ref[:, D_MODEL:D_MODEL + 1]
        w_hi = xs_ref[:, D_MODEL + 1:D_MODEL + 2]

        def ffn(wg, wu, wd, w):
            g = _dot(x, wg[0])
            u = _dot(x, wu[0])
            act = (g * _sigmoid(g)) * u * w
            return _dot(act.astype(BF16), wd[0])

        ys_ref[...] = ffn(wg_lo, wu_lo, wd_lo, w_lo) + ffn(wg_hi, wu_hi, wd_hi, w_hi)

    @pl.when(i >= meta_ref[0])
    def _():
        ys_ref[...] = jnp.zeros_like(ys_ref)


def _experts(meta, xs, e_gate, e_up, e_down, n_tiles_max):
    tm = xs.shape[0] // n_tiles_max

    def row_map(i, m):
        return (jnp.minimum(i, m[0] - 1), 0)

    def lo_map(i, m):
        return (m[1 + jnp.minimum(i, m[0] - 1)], 0, 0)

    def hi_map(i, m):
        return (m[1 + n_tiles_max + jnp.minimum(i, m[0] - 1)], 0, 0)

    up_spec = lambda mp: pl.BlockSpec((1, D_MODEL, D_EXPERT), mp)
    down_spec = lambda mp: pl.BlockSpec((1, D_EXPERT, D_MODEL), mp)
    return pl.pallas_call(
        _expert_kernel,
        grid_spec=pltpu.PrefetchScalarGridSpec(
            num_scalar_prefetch=1,
            grid=(n_tiles_max,),
            in_specs=[pl.BlockSpec((tm, ROW_W), row_map),
                      up_spec(lo_map), up_spec(lo_map), down_spec(lo_map),
                      up_spec(hi_map), up_spec(hi_map), down_spec(hi_map)],
            out_specs=pl.BlockSpec((tm, D_MODEL), lambda i, m: (i, 0)),
        ),
        out_shape=jax.ShapeDtypeStruct((xs.shape[0], D_MODEL), F32),
        compiler_params=pltpu.CompilerParams(
            dimension_semantics=("arbitrary",), vmem_limit_bytes=VMEM_LIMIT),
        name="experts",
    )(meta, xs, e_gate, e_up, e_down, e_gate, e_up, e_down)


def _combine_kernel(pos_ref, x1_ref, gf_ref, ys_ref, out_ref, ybuf, sem):
    tb = x1_ref.shape[0]
    base = pl.program_id(0) * tb

    def row_copy(r, p):
        return pltpu.make_async_copy(ys_ref.at[pl.ds(p, 1)], ybuf.at[pl.ds(r, 1)], sem)

    def issue(r, carry):
        row_copy(r, pos_ref[base + r]).start()
        return carry

    lax.fori_loop(0, tb, issue, 0)

    def drain(r, carry):
        row_copy(r, 0).wait()
        return carry

    lax.fori_loop(0, tb, drain, 0)
    out_ref[...] = _rms_norm(x1_ref[...] + ybuf[...], gf_ref[...])


def _combine(pos, x1, gf, ys):
    t = x1.shape[0]
    tb = min(TOKEN_TILE, t)
    assert t % tb == 0
    return pl.pallas_call(
        _combine_kernel,
        grid_spec=pltpu.PrefetchScalarGridSpec(
            num_scalar_prefetch=1,
            grid=(t // tb,),
            in_specs=[pl.BlockSpec((tb, D_MODEL), lambda i, pos: (i, 0)),
                      pl.BlockSpec((1, D_MODEL), lambda i, pos: (0, 0)),
                      pl.BlockSpec(memory_space=pl.ANY)],
            out_specs=pl.BlockSpec((tb, D_MODEL), lambda i, pos: (i, 0)),
            scratch_shapes=[pltpu.VMEM((tb, D_MODEL), F32), pltpu.SemaphoreType.DMA(())],
        ),
        out_shape=jax.ShapeDtypeStruct((t, D_MODEL), F32),
        compiler_params=pltpu.CompilerParams(
            dimension_semantics=("arbitrary",), vmem_limit_bytes=VMEM_LIMIT),
        name="combine",
    )(pos, x1, gf, ys)


def _block_diag(w, per_block):
    n, k, _ = w.shape
    nb = n // per_block
    w = w.reshape(nb, per_block, k, k)
    eye = jnp.eye(per_block, dtype=w.dtype)
    out = jnp.einsum('bpij,pq->bpiqj', w, eye)
    return out.reshape(nb, per_block * k, per_block * k)


def _pair_tables():
    lo, hi = [], []
    for g in range(N_GROUPS):
        for a in range(EXP_PER_GROUP):
            for b in range(a + 1, EXP_PER_GROUP):
                lo.append(g * EXP_PER_GROUP + a)
                hi.append(g * EXP_PER_GROUP + b)
    return jnp.array(lo, jnp.int32), jnp.array(hi, jnp.int32)


def kernel(x, norm1_g, w_in, pool_w, pool_scale, conv_w, conv_b, rg_w_r, rg_b_r, rg_w_i, rg_b_i, rg_lambda, proj_a, proj_b, w_out, norm2_g, router_group_w, router_group_b, router_expert_w, router_expert_b, exp_w_gate, exp_w_up, exp_w_down, norm_f_g):
    bsz, seq, d = x.shape
    t = bsz * seq
    assert w_in.shape[0] == 1, "single-layer block"
    l = 0
    row = lambda v: v.reshape(1, -1).astype(F32)
    heads_per_blk = MXU_DIM // (LRU_WIDTH // rg_w_r.shape[1])
    groups_per_blk = MXU_DIM // POOL_GROUP_DIM
    wrt = jnp.zeros((d, LANES), F32)
    wrt = wrt.at[:, 0:N_GROUPS].set(router_group_w[l]).at[:, N_GROUPS:N_GROUPS + N_EXP].set(router_expert_w[l])
    brt = jnp.zeros((1, LANES), F32)
    brt = brt.at[0, 0:N_GROUPS].set(router_group_b[l]).at[0, N_GROUPS:N_GROUPS + N_EXP].set(router_expert_b[l])

    x1, h2e, cnt = _mixer(
        x, row(norm1_g[l]), w_in[l].astype(BF16), _block_diag(pool_w[l], groups_per_blk).astype(BF16),
        row(pool_scale[l]), conv_w[l].astype(F32), row(conv_b[l]),
        _block_diag(rg_w_r[l], heads_per_blk).astype(BF16), _block_diag(rg_w_i[l], heads_per_blk).astype(BF16),
        row(rg_b_r[l]), row(rg_b_i[l]), row(rg_lambda[l]),
        proj_a[l].astype(BF16), proj_b[l].astype(BF16), w_out[l].astype(BF16), row(norm2_g[l]),
        wrt.astype(BF16), brt)

    tm = min(EXPERT_TILE, t)
    n_tiles_max = -(-(t + N_CLASSES * (tm - 1)) // tm)
    cls = h2e[:, D_MODEL + 2].astype(jnp.int32)
    rank = h2e[:, D_MODEL + 3].astype(jnp.int32)
    counts = cnt[0, 0:N_CLASSES].astype(jnp.int32)
    padded = ((counts + tm - 1) // tm) * tm
    ends = jnp.cumsum(padded)
    offs = ends - padded
    pos = rank + jnp.sum(jnp.where(cls[:, None] == jnp.arange(N_CLASSES)[None, :], offs[None, :], 0), axis=1)
    n_tiles = ends[-1] // tm
    tile_cls = jnp.minimum(
        jnp.sum((jnp.arange(n_tiles_max)[:, None] * tm >= ends[None, :]).astype(jnp.int32), axis=1),
        N_CLASSES - 1)
    pair_lo, pair_hi = _pair_tables()
    meta = jnp.concatenate([n_tiles[None], pair_lo[tile_cls], pair_hi[tile_cls]]).astype(jnp.int32)

    xs = _dispatch(pos, h2e, n_tiles_max * tm)
    ys = _experts(meta, xs, exp_w_gate[l].astype(BF16), exp_w_up[l].astype(BF16),
                  exp_w_down[l].astype(BF16), n_tiles_max)
    out = _combine(pos, x1.reshape(t, d), row(norm_f_g), ys)
    return out.reshape(bsz, seq, d)
```

```python
import jax
import jax.numpy as jnp
from jax import lax
from jax.experimental import pallas as pl
from jax.experimental.pallas import tpu as pltpu

F32 = jnp.float32
BF16 = jnp.bfloat16

D_MODEL = 1024
POOL_WIDTH = 512
POOL_GROUP_DIM = 128
POOL_WINDOWS = (2, 4, 8, 16)
POOL_HALO = 16
LRU_WIDTH = 1024
CONV_WIDTH = 4
CONV_HALO = 8
RG_C = 8.0
N_GROUPS = 4
EXP_PER_GROUP = 4
N_EXP = 16
D_EXPERT = 512
EPS = 1e-6
N_PAIRS = 6
N_CLASSES = N_GROUPS * N_PAIRS

LANES = 128
SUBLANES = 8
MXU_DIM = 256
META_W = LANES
ROW_W = D_MODEL + META_W
RT_ROWS = 32

O_POOL = 0
O_LRU = POOL_WIDTH
O_GATE = O_LRU + LRU_WIDTH
O_GA = O_GATE + LRU_WIDTH
O_GB = O_GA + D_MODEL
IN_COLS = O_GB + D_MODEL
O_GA_MAIN = O_GATE

SEQ_TILE = 256
POST_SPLIT = 1
N_SLABS = 2 * LRU_WIDTH // MXU_DIM
TOKEN_TILE = 512
EXPERT_TILE = 512
VMEM_LIMIT = 56 * 1024 * 1024


def _sigmoid(v):
    return 0.5 * jnp.tanh(0.5 * v) + 0.5


def _gelu_tanh(v):
    c = 0.7978845608028654
    return 0.5 * v * (1.0 + jnp.tanh(c * (v + 0.044715 * (v * v * v))))


def _rms_norm(v, g):
    ms = jnp.mean(v * v, axis=-1, keepdims=True)
    return v * lax.rsqrt(ms + EPS) * g


def _dot(a, b):
    return jnp.dot(a, b, preferred_element_type=F32)


def _dot_nt(a, b):
    return lax.dot_general(a, b, (((1,), (1,)), ((), ())), preferred_element_type=F32)


def _first_index_of_max(v, idx, sentinel):
    m = jnp.max(v, axis=0, keepdims=True)
    return m, jnp.min(jnp.where(v == m, idx, sentinel), axis=0, keepdims=True)


def _route(lt, counts):
    n = lt.shape[1]
    sub = lax.broadcasted_iota(jnp.int32, (EXP_PER_GROUP, n), 0).astype(F32)
    none = jnp.float32(EXP_PER_GROUP)
    neg = jnp.float32(-3.0e38)
    gl = lt[0:N_GROUPS, :]
    gmax, gidx = _first_index_of_max(gl, sub, none)
    p_top = 1.0 / jnp.sum(jnp.exp(gl - gmax), axis=0, keepdims=True)
    el = lt[N_GROUPS + (N_GROUPS - 1) * EXP_PER_GROUP:N_GROUPS + N_GROUPS * EXP_PER_GROUP, :]
    for g in range(N_GROUPS - 2, -1, -1):
        lo = N_GROUPS + g * EXP_PER_GROUP
        el = jnp.where(gidx == float(g), lt[lo:lo + EXP_PER_GROUP, :], el)
    v1, i1 = _first_index_of_max(el, sub, none)
    v2, i2 = _first_index_of_max(jnp.where(sub == i1, neg, el), sub, none)
    ex = jnp.exp(v2 - v1)
    w1 = p_top / (1.0 + ex)
    w2 = p_top * ex / (1.0 + ex)
    first_lo = i1 < i2
    w_lo = jnp.where(first_lo, w1, w2)
    w_hi = jnp.where(first_lo, w2, w1)
    pa_ = jnp.minimum(i1, i2)
    pb_ = jnp.maximum(i1, i2)
    cls = gidx * N_PAIRS + pa_ * (7.0 - pa_) * 0.5 + (pb_ - pa_ - 1.0)

    crow = lax.broadcasted_iota(jnp.int32, (RT_ROWS, n), 0).astype(F32)
    onehot = crow == cls
    t_row = lax.broadcasted_iota(jnp.int32, (n, n), 0)
    t_col = lax.broadcasted_iota(jnp.int32, (n, n), 1)
    earlier = jnp.where(t_row < t_col, 1.0, 0.0).astype(BF16)
    prior = _dot(jnp.where(onehot, 1.0, 0.0).astype(BF16), earlier) + counts
    rank = jnp.sum(jnp.where(onehot, prior, 0.0), axis=0, keepdims=True)
    counts = counts + jnp.sum(jnp.where(onehot, 1.0, 0.0), axis=1, keepdims=True)

    mrow = lax.broadcasted_iota(jnp.int32, (SUBLANES, n), 0)
    meta = jnp.where(mrow == 0, w_lo,
                     jnp.where(mrow == 1, w_hi,
                               jnp.where(mrow == 2, cls,
                                         jnp.where(mrow == 3, rank, 0.0))))
    return meta, counts


def _mixer_kernel(x_ref, g1_ref, w_in_ref, pool_bd_ref, pool_scale_ref, conv_w_ref, conv_b_ref,
                  wr_ref, wi_ref, br_ref, bi_ref, lam_ref, pa_ref, pb_ref, wo_ref, g2_ref,
                  wrt_ref, brt_ref, w_slab_ref,
                  x1_ref, h2e_ref, cnt_ref,
                  pool_ext, lru_ext, a_buf, b_buf, mixa_buf, h_buf, zg_buf, h_carry, cnt_scr):
    b = pl.program_id(0)
    c = pl.program_id(1)
    ts = x_ref.shape[1]

    @pl.when(c == 0)
    def _():
        pool_ext[0:POOL_HALO, :] = jnp.zeros((POOL_HALO, POOL_WIDTH), F32)
        lru_ext[0:CONV_HALO, :] = jnp.zeros((CONV_HALO, LRU_WIDTH), F32)
        h_carry[...] = jnp.zeros_like(h_carry)

    @pl.when((b == 0) & (c == 0))
    def _():
        cnt_scr[...] = jnp.zeros_like(cnt_scr)

    h = _rms_norm(x_ref[0], g1_ref[...]).astype(BF16)
    h_buf[...] = h

    pool_ext[POOL_HALO:POOL_HALO + ts, :] = _dot(h, w_in_ref[:, O_POOL:O_POOL + POOL_WIDTH])
    ext = pool_ext[...]
    pool_ext[0:POOL_HALO, :] = ext[ts:ts + POOL_HALO, :]
    pos1 = (c * ts + 1 + lax.broadcasted_iota(jnp.int32, (ts, POOL_GROUP_DIM), 0)).astype(F32)
    zs = []
    for g, w in enumerate(POOL_WINDOWS):
        s = ext[:, g * POOL_GROUP_DIM:(g + 1) * POOL_GROUP_DIM]
        u = s[POOL_HALO:]
        d = 1
        while d < w:
            s = s[d:] + s[:-d]
            d *= 2
        off = POOL_HALO - (w - 1)
        win = s[off:off + ts]
        zs.append(win / jnp.minimum(pos1, float(w)) - u)
    z = jnp.concatenate(zs, axis=1).astype(BF16)
    y_pool = jnp.concatenate(
        [_dot(z[:, j * MXU_DIM:(j + 1) * MXU_DIM], pool_bd_ref[j]) for j in range(POOL_WIDTH // MXU_DIM)],
        axis=1) * pool_scale_ref[...]
    y_a = _dot(y_pool.astype(BF16), pa_ref[...])
    mixa_buf[...] = _sigmoid(_dot(h, w_in_ref[:, O_GA_MAIN:O_GA_MAIN + D_MODEL])) * y_a

    lru_ext[CONV_HALO:CONV_HALO + ts, :] = _dot(h, w_in_ref[:, O_LRU:O_LRU + LRU_WIDTH])
    xc = conv_b_ref[...] + conv_w_ref[CONV_WIDTH - 1:CONV_WIDTH, :] * lru_ext[CONV_HALO:CONV_HALO + ts, :]
    for k in range(CONV_WIDTH - 1):
        o = CONV_HALO - (CONV_WIDTH - 1) + k
        xc = xc + conv_w_ref[k:k + 1, :] * lru_ext[o:o + ts, :]
    lru_ext[0:CONV_HALO, :] = lru_ext[ts:ts + CONV_HALO, :]
    xcb = xc.astype(BF16)
    nblk = LRU_WIDTH // MXU_DIM
    r = _sigmoid(jnp.concatenate(
        [_dot(xcb[:, j * MXU_DIM:(j + 1) * MXU_DIM], wr_ref[j]) for j in range(nblk)], axis=1) + br_ref[...])
    i = _sigmoid(jnp.concatenate(
        [_dot(xcb[:, j * MXU_DIM:(j + 1) * MXU_DIM], wi_ref[j]) for j in range(nblk)], axis=1) + bi_ref[...])
    nlam = -lam_ref[...]
    softplus = jnp.maximum(nlam, 0.0) + jnp.log1p(jnp.exp(-jnp.abs(nlam)))
    log_a = (-RG_C * softplus) * r
    a = jnp.exp(log_a)
    a_buf[...] = a
    v = jnp.tanh(-log_a) * (1.0 + a * a)
    b_buf[...] = jnp.where(v > 0.0, v * lax.rsqrt(v), 0.0) * (i * xc)

    row = lax.broadcasted_iota(jnp.int32, (SUBLANES, LRU_WIDTH), 0)

    def scan_body(gidx, carry):
        r0 = gidx * SUBLANES
        av = a_buf[pl.ds(r0, SUBLANES), :]
        bv = b_buf[pl.ds(r0, SUBLANES), :]
        for d in (1, 2, 4):
            a_sh = jnp.where(row >= d, pltpu.roll(av, d, 0), 1.0)
            b_sh = jnp.where(row >= d, pltpu.roll(bv, d, 0), 0.0)
            bv = bv + av * b_sh
            av = av * a_sh
        hv = bv + av * carry
        b_buf[pl.ds(r0, SUBLANES), :] = hv
        return hv[SUBLANES - 1:SUBLANES, :]

    groups_per_trip = ts // SUBLANES // N_SLABS
    carry = h_carry[0:1, :]
    for j in range(N_SLABS):
        zg_buf[j] = _dot(h_buf[...], w_slab_ref[j])
        for q in range(groups_per_trip):
            carry = scan_body(j * groups_per_trip + q, carry)
    h_carry[0:1, :] = carry

    hs = ts // POST_SPLIT
    counts = cnt_scr[...]
    for part in range(POST_SPLIT):
        rows = slice(part * hs, (part + 1) * hs)
        half = N_SLABS // 2
        z_gate = jnp.concatenate([zg_buf[j, rows, :] for j in range(half)], axis=1)
        z_gb = jnp.concatenate([zg_buf[half + j, rows, :] for j in range(half)], axis=1)
        y_lru = b_buf[rows, :] * _gelu_tanh(z_gate)
        y_b = _dot(y_lru.astype(BF16), pb_ref[...])
        mix = mixa_buf[rows, :] + _sigmoid(z_gb) * y_b
        x1 = x_ref[0, rows, :] + _dot(mix.astype(BF16), wo_ref[...])
        x1_ref[0, rows, :] = x1
        h2 = _rms_norm(x1, g2_ref[...])
        h2e_ref[rows, 0:D_MODEL] = h2
        lt = _dot_nt(wrt_ref[...], h2.astype(BF16)) + brt_ref[...]
        meta, counts = _route(lt, counts)
        meta_t = jnp.concatenate([meta, jnp.zeros((META_W - SUBLANES, hs), F32)], axis=0).T
        h2e_ref[rows, D_MODEL:ROW_W] = meta_t
    cnt_scr[...] = counts
    cnt_ref[...] = counts


def _const_spec(shape):
    zeros = (0,) * len(shape)
    return pl.BlockSpec(shape, lambda b, c: zeros, pipeline_mode=pl.Buffered(1))


def _mixer(x, g1, w_in, pool_bd, pool_scale, conv_w, conv_b, wr4, wi4, b_r, b_i, lam, pa, pb, wo, g2,
           wrt, brt, w_slab):
    bsz, seq, d = x.shape
    ts = min(SEQ_TILE, seq)
    hs = ts // POST_SPLIT
    assert seq % ts == 0 and ts % (SUBLANES * N_SLABS) == 0 and ts >= POOL_HALO and d == D_MODEL
    assert hs % LANES == 0 and brt.shape == (RT_ROWS, hs)
    nc = seq // ts
    consts = (g1, w_in, pool_bd, pool_scale, conv_w, conv_b, wr4, wi4, b_r, b_i, lam, pa, pb, wo, g2,
              wrt, brt, w_slab)
    return pl.pallas_call(
        _mixer_kernel,
        grid=(bsz, nc),
        in_specs=[pl.BlockSpec((1, ts, d), lambda b, c: (b, c, 0))] + [_const_spec(a.shape) for a in consts],
        out_specs=[
            pl.BlockSpec((1, ts, d), lambda b, c: (b, c, 0)),
            pl.BlockSpec((ts, ROW_W), lambda b, c: (b * nc + c, 0)),
            pl.BlockSpec((RT_ROWS, hs), lambda b, c: (0, 0)),
        ],
        out_shape=[
            jax.ShapeDtypeStruct((bsz, seq, d), F32),
            jax.ShapeDtypeStruct((bsz * seq, ROW_W), F32),
            jax.ShapeDtypeStruct((RT_ROWS, hs), F32),
        ],
        scratch_shapes=[
            pltpu.VMEM((POOL_HALO + ts, POOL_WIDTH), F32),
            pltpu.VMEM((CONV_HALO + ts, LRU_WIDTH), F32),
            pltpu.VMEM((ts, LRU_WIDTH), F32),
            pltpu.VMEM((ts, LRU_WIDTH), F32),
            pltpu.VMEM((ts, D_MODEL), F32),
            pltpu.VMEM((ts, D_MODEL), BF16),
            pltpu.VMEM((N_SLABS, ts, MXU_DIM), F32),
            pltpu.VMEM((SUBLANES, LRU_WIDTH), F32),
            pltpu.VMEM((RT_ROWS, hs), F32),
        ],
        compiler_params=pltpu.CompilerParams(
            dimension_semantics=("arbitrary", "arbitrary"), vmem_limit_bytes=VMEM_LIMIT),
        name="mixer",
    )(x, *consts)


def _dispatch_kernel(pos_ref, ztile_ref, h2e_ref, xs_ref, zbuf, sem, zsem):
    tb = h2e_ref.shape[0]
    tm = zbuf.shape[0]
    base = pl.program_id(0) * tb

    @pl.when(pl.program_id(0) == 0)
    def _():
        zbuf[...] = jnp.zeros_like(zbuf)

        def zero_copy(c):
            start = pl.multiple_of(ztile_ref[c], SUBLANES)
            return pltpu.make_async_copy(zbuf, xs_ref.at[pl.ds(start, tm)], zsem)

        for c in range(N_CLASSES):
            @pl.when(ztile_ref[c] >= 0)
            def _():
                zero_copy(c).start()
        for c in range(N_CLASSES):
            @pl.when(ztile_ref[c] >= 0)
            def _():
                zero_copy(c).wait()

        def tail_copy(k):
            return pltpu.make_async_copy(zbuf, xs_ref.at[pl.ds(pl.multiple_of(k * tm, SUBLANES), tm)], zsem)

        n_used = ztile_ref[N_CLASSES]
        n_all = xs_ref.shape[0] // tm
        lax.fori_loop(n_used, n_all, lambda k, c: (tail_copy(k).start(), c)[1], 0)
        lax.fori_loop(n_used, n_all, lambda k, c: (tail_copy(k).wait(), c)[1], 0)

    for r in range(tb):
        pltpu.make_async_copy(
            h2e_ref.at[pl.ds(r, 1)], xs_ref.at[pl.ds(pos_ref[base + r], 1)], sem).start()
    pltpu.make_async_copy(h2e_ref, xs_ref.at[pl.ds(0, tb)], sem).wait()


def _dispatch(pos, ztile, h2e, n_slots, tm):
    t = h2e.shape[0]
    tb = min(TOKEN_TILE, t)
    assert t % tb == 0 and n_slots >= tb
    return pl.pallas_call(
        _dispatch_kernel,
        grid_spec=pltpu.PrefetchScalarGridSpec(
            num_scalar_prefetch=2,
            grid=(t // tb,),
            in_specs=[pl.BlockSpec((tb, ROW_W), lambda i, pos, zt: (i, 0))],
            out_specs=pl.BlockSpec(memory_space=pl.ANY),
            scratch_shapes=[pltpu.VMEM((tm, ROW_W), F32),
                            pltpu.SemaphoreType.DMA(()), pltpu.SemaphoreType.DMA(())],
        ),
        out_shape=jax.ShapeDtypeStruct((n_slots, ROW_W), F32),
        compiler_params=pltpu.CompilerParams(
            dimension_semantics=("arbitrary",), vmem_limit_bytes=VMEM_LIMIT),
        name="dispatch",
    )(pos, ztile, h2e)


def _expert_kernel(meta_ref, xs_ref, wg_lo, wu_lo, wd_lo, wg_hi, wu_hi, wd_hi, ys_ref):
    i = pl.program_id(0)

    @pl.when(i < meta_ref[0])
    def _():
        x = xs_ref[:, 0:D_MODEL].astype(BF16)
        w_lo = xs_ref[:, D_MODEL:D_MODEL + 1]
        w_hi = xs_ref[:, D_MODEL + 1:D_MODEL + 2]

        def ffn(wg, wu, wd, w):
            g = _dot(x, wg[0])
            u = _dot(x, wu[0])
            act = (g * _sigmoid(g)) * u * w
            return _dot(act.astype(BF16), wd[0])

        ys_ref[...] = ffn(wg_lo, wu_lo, wd_lo, w_lo) + ffn(wg_hi, wu_hi, wd_hi, w_hi)

    @pl.when(i >= meta_ref[0])
    def _():
        ys_ref[...] = jnp.zeros_like(ys_ref)


def _experts(meta, xs, e_gate, e_up, e_down, n_tiles_max):
    tm = xs.shape[0] // n_tiles_max

    def row_map(i, m):
        return (jnp.minimum(i, m[0] - 1), 0)

    def lo_map(i, m):
        return (m[1 + jnp.minimum(i, m[0] - 1)], 0, 0)

    def hi_map(i, m):
        return (m[1 + n_tiles_max + jnp.minimum(i, m[0] - 1)], 0, 0)

    up_spec = lambda mp: pl.BlockSpec((1, D_MODEL, D_EXPERT), mp)
    down_spec = lambda mp: pl.BlockSpec((1, D_EXPERT, D_MODEL), mp)
    return pl.pallas_call(
        _expert_kernel,
        grid_spec=pltpu.PrefetchScalarGridSpec(
            num_scalar_prefetch=1,
            grid=(n_tiles_max,),
            in_specs=[pl.BlockSpec((tm, ROW_W), row_map),
                      up_spec(lo_map), up_spec(lo_map), down_spec(lo_map),
                      up_spec(hi_map), up_spec(hi_map), down_spec(hi_map)],
            out_specs=pl.BlockSpec((tm, D_MODEL), lambda i, m: (i, 0)),
        ),
        out_shape=jax.ShapeDtypeStruct((xs.shape[0], D_MODEL), F32),
        compiler_params=pltpu.CompilerParams(
            dimension_semantics=("arbitrary",), vmem_limit_bytes=VMEM_LIMIT),
        name="experts",
    )(meta, xs, e_gate, e_up, e_down, e_gate, e_up, e_down)


def _combine_kernel(pos_ref, x1_ref, gf_ref, ys_ref, out_ref, ybuf, sems):
    tb = x1_ref.shape[0]
    i = pl.program_id(0)
    slot = i % 2

    def gather(step, dst_slot):
        base = step * tb
        for r in range(tb):
            pltpu.make_async_copy(ys_ref.at[pl.ds(pos_ref[base + r], 1)],
                                  ybuf.at[dst_slot, pl.ds(r, 1)], sems.at[dst_slot]).start()

    @pl.when(i == 0)
    def _():
        gather(0, 0)

    @pl.when(i + 1 < pl.num_programs(0))
    def _():
        gather(i + 1, 1 - slot)

    pltpu.make_async_copy(ys_ref.at[pl.ds(0, tb)], ybuf.at[slot], sems.at[slot]).wait()
    out_ref[...] = _rms_norm(x1_ref[...] + ybuf[slot], gf_ref[...])


def _combine(pos, x1, gf, ys):
    t = x1.shape[0]
    tb = min(TOKEN_TILE, t)
    assert t % tb == 0
    return pl.pallas_call(
        _combine_kernel,
        grid_spec=pltpu.PrefetchScalarGridSpec(
            num_scalar_prefetch=1,
            grid=(t // tb,),
            in_specs=[pl.BlockSpec((tb, D_MODEL), lambda i, pos: (i, 0)),
                      pl.BlockSpec((1, D_MODEL), lambda i, pos: (0, 0)),
                      pl.BlockSpec(memory_space=pl.ANY)],
            out_specs=pl.BlockSpec((tb, D_MODEL), lambda i, pos: (i, 0)),
            scratch_shapes=[pltpu.VMEM((2, tb, D_MODEL), F32), pltpu.SemaphoreType.DMA((2,))],
        ),
        out_shape=jax.ShapeDtypeStruct((t, D_MODEL), F32),
        compiler_params=pltpu.CompilerParams(
            dimension_semantics=("arbitrary",), vmem_limit_bytes=VMEM_LIMIT),
        name="combine",
    )(pos, x1, gf, ys)


def _block_diag(w, per_block):
    n, k, _ = w.shape
    nb = n // per_block
    w = w.reshape(nb, per_block, k, k)
    eye = jnp.eye(per_block, dtype=w.dtype)
    out = jnp.einsum('bpij,pq->bpiqj', w, eye)
    return out.reshape(nb, per_block * k, per_block * k)


def _pair_tables():
    lo, hi = [], []
    for g in range(N_GROUPS):
        for a in range(EXP_PER_GROUP):
            for b in range(a + 1, EXP_PER_GROUP):
                lo.append(g * EXP_PER_GROUP + a)
                hi.append(g * EXP_PER_GROUP + b)
    return jnp.array(lo, jnp.int32), jnp.array(hi, jnp.int32)


def kernel(x, norm1_g, w_in, pool_w, pool_scale, conv_w, conv_b, rg_w_r, rg_b_r, rg_w_i, rg_b_i, rg_lambda, proj_a, proj_b, w_out, norm2_g, router_group_w, router_group_b, router_expert_w, router_expert_b, exp_w_gate, exp_w_up, exp_w_down, norm_f_g):
    bsz, seq, d = x.shape
    t = bsz * seq
    assert w_in.shape[0] == 1, "single-layer block"
    l = 0
    row = lambda v: v.reshape(1, -1).astype(F32)
    heads_per_blk = MXU_DIM // (LRU_WIDTH // rg_w_r.shape[1])
    groups_per_blk = MXU_DIM // POOL_GROUP_DIM
    n_rt = N_GROUPS + N_EXP
    wrt = jnp.concatenate([router_group_w[l].T, router_expert_w[l].T,
                           jnp.zeros((RT_ROWS - n_rt, d), F32)], axis=0)
    brt = jnp.concatenate([router_group_b[l], router_expert_b[l], jnp.zeros((RT_ROWS - n_rt,), F32)])
    hs = min(SEQ_TILE, seq) // POST_SPLIT
    brt = jnp.broadcast_to(brt[:, None].astype(F32), (RT_ROWS, hs))

    w_in_b = w_in[l].astype(BF16)
    w_slab = jnp.concatenate([w_in_b[:, O_GATE:O_GATE + LRU_WIDTH], w_in_b[:, O_GB:O_GB + D_MODEL]], axis=1)
    w_slab = w_slab.reshape(d, N_SLABS, MXU_DIM).transpose(1, 0, 2)

    x1, h2e, cnt = _mixer(
        x, row(norm1_g[l]), jnp.concatenate([w_in_b[:, 0:O_GATE], w_in_b[:, O_GA:O_GB]], axis=1),
        _block_diag(pool_w[l], groups_per_blk).astype(BF16),
        row(pool_scale[l]), conv_w[l].astype(F32), row(conv_b[l]),
        _block_diag(rg_w_r[l], heads_per_blk).astype(BF16), _block_diag(rg_w_i[l], heads_per_blk).astype(BF16),
        row(rg_b_r[l]), row(rg_b_i[l]), row(rg_lambda[l]),
        proj_a[l].astype(BF16), proj_b[l].astype(BF16), w_out[l].astype(BF16), row(norm2_g[l]),
        wrt.astype(BF16), brt, w_slab)

    tm = min(EXPERT_TILE, t)
    n_tiles_max = -(-(t + N_CLASSES * (tm - 1)) // tm)
    cls = h2e[:, D_MODEL + 2].astype(jnp.int32)
    rank = h2e[:, D_MODEL + 3].astype(jnp.int32)
    counts = cnt[0:N_CLASSES, 0].astype(jnp.int32)
    padded = ((counts + tm - 1) // tm) * tm
    ends = jnp.cumsum(padded)
    offs = ends - padded
    pos = rank + jnp.sum(jnp.where(cls[:, None] == jnp.arange(N_CLASSES)[None, :], offs[None, :], 0), axis=1)
    n_tiles = ends[-1] // tm
    tile_cls = jnp.minimum(
        jnp.sum((jnp.arange(n_tiles_max)[:, None] * tm >= ends[None, :]).astype(jnp.int32), axis=1),
        N_CLASSES - 1)
    pair_lo, pair_hi = _pair_tables()
    meta = jnp.concatenate([n_tiles[None], pair_lo[tile_cls], pair_hi[tile_cls]]).astype(jnp.int32)

    ztile = jnp.concatenate([jnp.where(padded > 0, ends - tm, -1), n_tiles[None]]).astype(jnp.int32)
    xs = _dispatch(pos, ztile, h2e, n_tiles_max * tm, tm)
    ys = _experts(meta, xs, exp_w_gate[l].astype(BF16), exp_w_up[l].astype(BF16),
                  exp_w_down[l].astype(BF16), n_tiles_max)
    out = _combine(pos, x1.reshape(t, d), row(norm_f_g), ys)
    return out.reshape(bsz, seq, d)
```

```python
import jax
import jax.numpy as jnp
from jax import lax
from jax.experimental import pallas as pl
from jax.experimental.pallas import tpu as pltpu

F32 = jnp.float32
BF16 = jnp.bfloat16

LANES = 128
SUBLANES = 8
MXU_DIM = 256
NB = SUBLANES

D_MODEL = 1024
POOL_WIDTH = 512
POOL_GROUP_DIM = 128
POOL_WINDOWS = (2, 4, 8, 16)
POOL_HALO = max(POOL_WINDOWS) * NB
LRU_WIDTH = 1024
CONV_WIDTH = 4
CONV_HALO = (CONV_WIDTH - 1) * NB
RG_C = 8.0
N_GROUPS = 4
EXP_PER_GROUP = 4
N_EXP = 16
D_EXPERT = 512
EPS = 1e-6
N_PAIRS = 6
N_CLASSES = N_GROUPS * N_PAIRS

META_W = LANES
ROW_W = D_MODEL + META_W
RT_ROWS = 32

O_POOL = 0
O_LRU = POOL_WIDTH
O_GATE = O_LRU + LRU_WIDTH
O_GA = O_GATE + LRU_WIDTH
O_GB = O_GA + D_MODEL
IN_COLS = O_GB + D_MODEL
O_GA_MAIN = O_GATE

SEQ_TILE = 512
POST_SPLIT = 1
N_SLABS = 2 * LRU_WIDTH // MXU_DIM
TOKEN_TILE = 512
EXPERT_TILE = 512
VMEM_LIMIT = 56 * 1024 * 1024


def _sigmoid(v):
    return 0.5 * jnp.tanh(0.5 * v) + 0.5


def _gelu_tanh(v):
    c = 0.7978845608028654
    return 0.5 * v * (1.0 + jnp.tanh(c * (v + 0.044715 * (v * v * v))))


def _rms_norm(v, g):
    ms = jnp.mean(v * v, axis=-1, keepdims=True)
    return v * lax.rsqrt(ms + EPS) * g


def _dot(a, b):
    return jnp.dot(a, b, preferred_element_type=F32)


def _dot_nt(a, b):
    return lax.dot_general(a, b, (((1,), (1,)), ((), ())), preferred_element_type=F32)


def _first_index_of_max(v, idx, sentinel):
    m = jnp.max(v, axis=0, keepdims=True)
    return m, jnp.min(jnp.where(v == m, idx, sentinel), axis=0, keepdims=True)


def _route(lt, counts):
    n = lt.shape[1]
    sub = lax.broadcasted_iota(jnp.int32, (EXP_PER_GROUP, n), 0).astype(F32)
    none = jnp.float32(EXP_PER_GROUP)
    neg = jnp.float32(-3.0e38)
    gl = lt[0:N_GROUPS, :]
    gmax, gidx = _first_index_of_max(gl, sub, none)
    p_top = 1.0 / jnp.sum(jnp.exp(gl - gmax), axis=0, keepdims=True)
    el = lt[N_GROUPS + (N_GROUPS - 1) * EXP_PER_GROUP:N_GROUPS + N_GROUPS * EXP_PER_GROUP, :]
    for g in range(N_GROUPS - 2, -1, -1):
        lo = N_GROUPS + g * EXP_PER_GROUP
        el = jnp.where(gidx == float(g), lt[lo:lo + EXP_PER_GROUP, :], el)
    v1, i1 = _first_index_of_max(el, sub, none)
    v2, i2 = _first_index_of_max(jnp.where(sub == i1, neg, el), sub, none)
    ex = jnp.exp(v2 - v1)
    w1 = p_top / (1.0 + ex)
    w2 = p_top * ex / (1.0 + ex)
    first_lo = i1 < i2
    w_lo = jnp.where(first_lo, w1, w2)
    w_hi = jnp.where(first_lo, w2, w1)
    pa_ = jnp.minimum(i1, i2)
    pb_ = jnp.maximum(i1, i2)
    cls = gidx * N_PAIRS + pa_ * (7.0 - pa_) * 0.5 + (pb_ - pa_ - 1.0)

    crow = lax.broadcasted_iota(jnp.int32, (RT_ROWS, n), 0).astype(F32)
    onehot = crow == cls
    t_row = lax.broadcasted_iota(jnp.int32, (n, n), 0)
    t_col = lax.broadcasted_iota(jnp.int32, (n, n), 1)
    earlier = jnp.where(t_row < t_col, 1.0, 0.0).astype(BF16)
    prior = _dot(jnp.where(onehot, 1.0, 0.0).astype(BF16), earlier) + counts
    rank = jnp.sum(jnp.where(onehot, prior, 0.0), axis=0, keepdims=True)
    counts = counts + jnp.sum(jnp.where(onehot, 1.0, 0.0), axis=1, keepdims=True)

    mrow = lax.broadcasted_iota(jnp.int32, (SUBLANES, n), 0)
    meta = jnp.where(mrow == 0, w_lo,
                     jnp.where(mrow == 1, w_hi,
                               jnp.where(mrow == 2, cls,
                                         jnp.where(mrow == 3, rank, 0.0))))
    return meta, counts


def _mixer_kernel(x_hbm, g1_ref, w_in_ref, pool_bd_ref, pool_scale_ref, conv_w_ref, conv_b_ref,
                  wr_ref, wi_ref, br_ref, bi_ref, lam_ref, pa_ref, pb_ref, wo_ref, g2_ref,
                  wrt_ref, brt_ref, w_slab_ref,
                  x1_ref, h2e_ref, cnt_ref,
                  xbuf, xsem, pool_ext, lru_ext, a_buf, b_buf, mixa_buf, h_buf, zg_buf, h_carry, cnt_scr):
    g = pl.program_id(0)
    c = pl.program_id(1)
    nc = pl.num_programs(1)
    tq = xbuf.shape[1]
    ts = tq * NB
    step = g * nc + c
    slot = step % 2

    def x_copies(g_, c_, slot_):
        return [pltpu.make_async_copy(x_hbm.at[g_ * NB + bb, pl.ds(c_ * tq, tq), :],
                                      xbuf.at[slot_, :, bb, :], xsem.at[slot_]) for bb in range(NB)]

    @pl.when(step == 0)
    def _():
        for cp in x_copies(g, c, slot):
            cp.start()

    @pl.when(step + 1 < pl.num_programs(0) * nc)
    def _():
        wrap = c + 1 == nc
        for cp in x_copies(jnp.where(wrap, g + 1, g), jnp.where(wrap, 0, c + 1), 1 - slot):
            cp.start()

    @pl.when(c == 0)
    def _():
        pool_ext[0:POOL_HALO, :] = jnp.zeros((POOL_HALO, POOL_WIDTH), F32)
        lru_ext[0:CONV_HALO, :] = jnp.zeros((CONV_HALO, LRU_WIDTH), F32)
        h_carry[...] = jnp.zeros_like(h_carry)

    @pl.when(step == 0)
    def _():
        cnt_scr[...] = jnp.zeros_like(cnt_scr)

    for cp in x_copies(g, c, slot):
        cp.wait()
    x = xbuf[slot].reshape(ts, D_MODEL)
    h = _rms_norm(x, g1_ref[...]).astype(BF16)
    h_buf[...] = h

    pool_ext[POOL_HALO:POOL_HALO + ts, :] = _dot(h, w_in_ref[:, O_POOL:O_POOL + POOL_WIDTH])
    ext = pool_ext[...]
    pool_ext[0:POOL_HALO, :] = ext[ts:ts + POOL_HALO, :]
    t_idx = lax.broadcasted_iota(jnp.int32, (ts, POOL_GROUP_DIM), 0) // NB
    pos1 = (c * tq + 1 + t_idx).astype(F32)
    zs = []
    for g, w in enumerate(POOL_WINDOWS):
        s = ext[:, g * POOL_GROUP_DIM:(g + 1) * POOL_GROUP_DIM]
        u = s[POOL_HALO:]
        d = 1
        while d < w:
            s = s[d * NB:] + s[:-d * NB]
            d *= 2
        off = POOL_HALO - (w - 1) * NB
        win = s[off:off + ts]
        zs.append(win / jnp.minimum(pos1, float(w)) - u)
    z = jnp.concatenate(zs, axis=1).astype(BF16)
    y_pool = jnp.concatenate(
        [_dot(z[:, j * MXU_DIM:(j + 1) * MXU_DIM], pool_bd_ref[j]) for j in range(POOL_WIDTH // MXU_DIM)],
        axis=1) * pool_scale_ref[...]
    y_a = _dot(y_pool.astype(BF16), pa_ref[...])
    mixa_buf[...] = _sigmoid(_dot(h, w_in_ref[:, O_GA_MAIN:O_GA_MAIN + D_MODEL])) * y_a

    lru_ext[CONV_HALO:CONV_HALO + ts, :] = _dot(h, w_in_ref[:, O_LRU:O_LRU + LRU_WIDTH])
    xc = conv_b_ref[...] + conv_w_ref[CONV_WIDTH - 1:CONV_WIDTH, :] * lru_ext[CONV_HALO:CONV_HALO + ts, :]
    for k in range(CONV_WIDTH - 1):
        o = k * NB
        xc = xc + conv_w_ref[k:k + 1, :] * lru_ext[o:o + ts, :]
    lru_ext[0:CONV_HALO, :] = lru_ext[ts:ts + CONV_HALO, :]
    xcb = xc.astype(BF16)
    nblk = LRU_WIDTH // MXU_DIM
    r = _sigmoid(jnp.concatenate(
        [_dot(xcb[:, j * MXU_DIM:(j + 1) * MXU_DIM], wr_ref[j]) for j in range(nblk)], axis=1) + br_ref[...])
    i = _sigmoid(jnp.concatenate(
        [_dot(xcb[:, j * MXU_DIM:(j + 1) * MXU_DIM], wi_ref[j]) for j in range(nblk)], axis=1) + bi_ref[...])
    nlam = -lam_ref[...]
    softplus = jnp.maximum(nlam, 0.0) + jnp.log1p(jnp.exp(-jnp.abs(nlam)))
    log_a = (-RG_C * softplus) * r
    a = jnp.exp(log_a)
    a_buf[...] = a
    v = jnp.tanh(-log_a) * (1.0 + a * a)
    b_buf[...] = jnp.where(v > 0.0, v * lax.rsqrt(v), 0.0) * (i * xc)

    steps_per_trip = tq // N_SLABS
    carry = h_carry[...]
    for j in range(N_SLABS):
        zg_buf[j] = _dot(h_buf[...], w_slab_ref[j])
        for q in range(steps_per_trip):
            r0 = (j * steps_per_trip + q) * NB
            carry = b_buf[r0:r0 + NB, :] + a_buf[r0:r0 + NB, :] * carry
            b_buf[r0:r0 + NB, :] = carry
    h_carry[...] = carry

    hs = ts // POST_SPLIT
    counts = cnt_scr[...]
    for part in range(POST_SPLIT):
        rows = slice(part * hs, (part + 1) * hs)
        half = N_SLABS // 2
        z_gate = jnp.concatenate([zg_buf[j, rows, :] for j in range(half)], axis=1)
        z_gb = jnp.concatenate([zg_buf[half + j, rows, :] for j in range(half)], axis=1)
        y_lru = b_buf[rows, :] * _gelu_tanh(z_gate)
        y_b = _dot(y_lru.astype(BF16), pb_ref[...])
        mix = mixa_buf[rows, :] + _sigmoid(z_gb) * y_b
        x1 = x[rows, :] + _dot(mix.astype(BF16), wo_ref[...])
        x1_ref[rows, :] = x1
        h2 = _rms_norm(x1, g2_ref[...])
        h2e_ref[rows, 0:D_MODEL] = h2
        lt = _dot_nt(wrt_ref[...], h2.astype(BF16)) + brt_ref[...]
        meta, counts = _route(lt, counts)
        meta_t = jnp.concatenate([meta, jnp.zeros((META_W - SUBLANES, hs), F32)], axis=0).T
        h2e_ref[rows, D_MODEL:ROW_W] = meta_t
    cnt_scr[...] = counts
    cnt_ref[...] = counts


def _const_spec(shape):
    zeros = (0,) * len(shape)
    return pl.BlockSpec(shape, lambda b, c: zeros, pipeline_mode=pl.Buffered(1))


def _mixer_tiles(bsz, seq):
    tq = min(SEQ_TILE // NB, seq)
    ts = tq * NB
    assert bsz % NB == 0 and seq % tq == 0 and tq % N_SLABS == 0 and ts >= POOL_HALO
    return ts, tq


def _mixer(x, g1, w_in, pool_bd, pool_scale, conv_w, conv_b, wr4, wi4, b_r, b_i, lam, pa, pb, wo, g2,
           wrt, brt, w_slab):
    bsz, seq, d = x.shape
    ts, tq = _mixer_tiles(bsz, seq)
    hs = ts // POST_SPLIT
    assert d == D_MODEL and hs % LANES == 0 and brt.shape == (RT_ROWS, hs)
    nc = seq // tq
    consts = (g1, w_in, pool_bd, pool_scale, conv_w, conv_b, wr4, wi4, b_r, b_i, lam, pa, pb, wo, g2,
              wrt, brt, w_slab)
    return pl.pallas_call(
        _mixer_kernel,
        grid=(bsz // NB, nc),
        in_specs=[pl.BlockSpec(memory_space=pl.ANY)] + [_const_spec(a.shape) for a in consts],
        out_specs=[
            pl.BlockSpec((ts, d), lambda g, c: (g * nc + c, 0)),
            pl.BlockSpec((ts, ROW_W), lambda g, c: (g * nc + c, 0)),
            pl.BlockSpec((RT_ROWS, hs), lambda g, c: (0, 0)),
        ],
        out_shape=[
            jax.ShapeDtypeStruct((bsz * seq, d), F32),
            jax.ShapeDtypeStruct((bsz * seq, ROW_W), F32),
            jax.ShapeDtypeStruct((RT_ROWS, hs), F32),
        ],
        scratch_shapes=[
            pltpu.VMEM((2, tq, NB, d), F32),
            pltpu.SemaphoreType.DMA((2,)),
            pltpu.VMEM((POOL_HALO + ts, POOL_WIDTH), F32),
            pltpu.VMEM((CONV_HALO + ts, LRU_WIDTH), F32),
            pltpu.VMEM((ts, LRU_WIDTH), F32),
            pltpu.VMEM((ts, LRU_WIDTH), F32),
            pltpu.VMEM((ts, D_MODEL), F32),
            pltpu.VMEM((ts, D_MODEL), BF16),
            pltpu.VMEM((N_SLABS, ts, MXU_DIM), F32),
            pltpu.VMEM((NB, LRU_WIDTH), F32),
            pltpu.VMEM((RT_ROWS, hs), F32),
        ],
        compiler_params=pltpu.CompilerParams(
            dimension_semantics=("arbitrary", "arbitrary"), vmem_limit_bytes=VMEM_LIMIT),
        name="mixer",
    )(x, *consts)


def _dispatch_kernel(pos_ref, ztile_ref, h2e_ref, xs_ref, zbuf, sem, zsem):
    tb = h2e_ref.shape[0]
    tm = zbuf.shape[0]
    base = pl.program_id(0) * tb

    @pl.when(pl.program_id(0) == 0)
    def _():
        zbuf[...] = jnp.zeros_like(zbuf)

        def zero_copy(c):
            start = pl.multiple_of(ztile_ref[c], SUBLANES)
            return pltpu.make_async_copy(zbuf, xs_ref.at[pl.ds(start, tm)], zsem)

        for c in range(N_CLASSES):
            @pl.when(ztile_ref[c] >= 0)
            def _():
                zero_copy(c).start()
        for c in range(N_CLASSES):
            @pl.when(ztile_ref[c] >= 0)
            def _():
                zero_copy(c).wait()

        def tail_copy(k):
            return pltpu.make_async_copy(zbuf, xs_ref.at[pl.ds(pl.multiple_of(k * tm, SUBLANES), tm)], zsem)

        n_used = ztile_ref[N_CLASSES]
        n_all = xs_ref.shape[0] // tm
        lax.fori_loop(n_used, n_all, lambda k, c: (tail_copy(k).start(), c)[1], 0)
        lax.fori_loop(n_used, n_all, lambda k, c: (tail_copy(k).wait(), c)[1], 0)

    for r in range(tb):
        pltpu.make_async_copy(
            h2e_ref.at[pl.ds(r, 1)], xs_ref.at[pl.ds(pos_ref[base + r], 1)], sem).start()
    pltpu.make_async_copy(h2e_ref, xs_ref.at[pl.ds(0, tb)], sem).wait()


def _dispatch(pos, ztile, h2e, n_slots, tm):
    t = h2e.shape[0]
    tb = min(TOKEN_TILE, t)
    assert t % tb == 0 and n_slots >= tb
    return pl.pallas_call(
        _dispatch_kernel,
        grid_spec=pltpu.PrefetchScalarGridSpec(
            num_scalar_prefetch=2,
            grid=(t // tb,),
            in_specs=[pl.BlockSpec((tb, ROW_W), lambda i, pos, zt: (i, 0))],
            out_specs=pl.BlockSpec(memory_space=pl.ANY),
            scratch_shapes=[pltpu.VMEM((tm, ROW_W), F32),
                            pltpu.SemaphoreType.DMA(()), pltpu.SemaphoreType.DMA(())],
        ),
        out_shape=jax.ShapeDtypeStruct((n_slots, ROW_W), F32),
        compiler_params=pltpu.CompilerParams(
            dimension_semantics=("arbitrary",), vmem_limit_bytes=VMEM_LIMIT),
        name="dispatch",
    )(pos, ztile, h2e)


def _expert_kernel(meta_ref, xs_ref, wg_lo, wu_lo, wd_lo, wg_hi, wu_hi, wd_hi, ys_ref):
    i = pl.program_id(0)

    @pl.when(i < meta_ref[0])
    def _():
        x = xs_ref[:, 0:D_MODEL].astype(BF16)
        w_lo = xs_ref[:, D_MODEL:D_MODEL + 1]
        w_hi = xs_ref[:, D_MODEL + 1:D_MODEL + 2]

        def ffn(wg, wu, wd, w):
            g = _dot(x, wg[0])
            u = _dot(x, wu[0])
            act = (g * _sigmoid(g)) * u * w
            return _dot(act.astype(BF16), wd[0])

        ys_ref[...] = ffn(wg_lo, wu_lo, wd_lo, w_lo) + ffn(wg_hi, wu_hi, wd_hi, w_hi)

    @pl.when(i >= meta_ref[0])
    def _():
        ys_ref[...] = jnp.zeros_like(ys_ref)


def _experts(meta, xs, e_gate, e_up, e_down, n_tiles_max):
    tm = xs.shape[0] // n_tiles_max

    def row_map(i, m):
        return (jnp.minimum(i, m[0] - 1), 0)

    def lo_map(i, m):
        return (m[1 + jnp.minimum(i, m[0] - 1)], 0, 0)

    def hi_map(i, m):
        return (m[1 + n_tiles_max + jnp.minimum(i, m[0] - 1)], 0, 0)

    up_spec = lambda mp: pl.BlockSpec((1, D_MODEL, D_EXPERT), mp)
    down_spec = lambda mp: pl.BlockSpec((1, D_EXPERT, D_MODEL), mp)
    return pl.pallas_call(
        _expert_kernel,
        grid_spec=pltpu.PrefetchScalarGridSpec(
            num_scalar_prefetch=1,
            grid=(n_tiles_max,),
            in_specs=[pl.BlockSpec((tm, ROW_W), row_map),
                      up_spec(lo_map), up_spec(lo_map), down_spec(lo_map),
                      up_spec(hi_map), up_spec(hi_map), down_spec(hi_map)],
            out_specs=pl.BlockSpec((tm, D_MODEL), lambda i, m: (i, 0)),
        ),
        out_shape=jax.ShapeDtypeStruct((xs.shape[0], D_MODEL), F32),
        compiler_params=pltpu.CompilerParams(
            dimension_semantics=("arbitrary",), vmem_limit_bytes=VMEM_LIMIT),
        name="experts",
    )(meta, xs, e_gate, e_up, e_down, e_gate, e_up, e_down)


def _combine_kernel(pos_ref, x1_ref, gf_ref, ys_ref, out_hbm, ybuf, sems, obuf, osems):
    tb = x1_ref.shape[0]
    tq = obuf.shape[1]
    nc = out_hbm.shape[1] // tq
    i = pl.program_id(0)
    slot = i % 2

    def out_copies(step, src_slot):
        g_, c_ = step // nc, step % nc
        return [pltpu.make_async_copy(obuf.at[src_slot, :, bb, :],
                                      out_hbm.at[g_ * NB + bb, pl.ds(c_ * tq, tq), :],
                                      osems.at[src_slot]) for bb in range(NB)]

    def gather(step, dst_slot):
        base = step * tb
        for r in range(tb):
            pltpu.make_async_copy(ys_ref.at[pl.ds(pos_ref[base + r], 1)],
                                  ybuf.at[dst_slot, pl.ds(r, 1)], sems.at[dst_slot]).start()

    @pl.when(i == 0)
    def _():
        gather(0, 0)

    @pl.when(i + 1 < pl.num_programs(0))
    def _():
        gather(i + 1, 1 - slot)

    pltpu.make_async_copy(ys_ref.at[pl.ds(0, tb)], ybuf.at[slot], sems.at[slot]).wait()

    @pl.when(i >= 2)
    def _():
        for cp in out_copies(i - 2, slot):
            cp.wait()

    obuf[slot] = _rms_norm(x1_ref[...] + ybuf[slot], gf_ref[...]).reshape(tq, NB, D_MODEL)
    for cp in out_copies(i, slot):
        cp.start()

    @pl.when(i + 1 == pl.num_programs(0))
    def _():
        @pl.when(i >= 1)
        def _():
            for cp in out_copies(i - 1, 1 - slot):
                cp.wait()
        for cp in out_copies(i, slot):
            cp.wait()


def _combine(pos, x1, gf, ys, bsz, seq):
    t = x1.shape[0]
    tb, tq = _mixer_tiles(bsz, seq)
    assert t % tb == 0
    return pl.pallas_call(
        _combine_kernel,
        grid_spec=pltpu.PrefetchScalarGridSpec(
            num_scalar_prefetch=1,
            grid=(t // tb,),
            in_specs=[pl.BlockSpec((tb, D_MODEL), lambda i, pos: (i, 0)),
                      pl.BlockSpec((1, D_MODEL), lambda i, pos: (0, 0)),
                      pl.BlockSpec(memory_space=pl.ANY)],
            out_specs=pl.BlockSpec(memory_space=pl.ANY),
            scratch_shapes=[pltpu.VMEM((2, tb, D_MODEL), F32), pltpu.SemaphoreType.DMA((2,)),
                            pltpu.VMEM((2, tq, NB, D_MODEL), F32), pltpu.SemaphoreType.DMA((2,))],
        ),
        out_shape=jax.ShapeDtypeStruct((bsz, seq, D_MODEL), F32),
        compiler_params=pltpu.CompilerParams(
            dimension_semantics=("arbitrary",), vmem_limit_bytes=VMEM_LIMIT),
        name="combine",
    )(pos, x1, gf, ys)


def _block_diag(w, per_block):
    n, k, _ = w.shape
    nb = n // per_block
    w = w.reshape(nb, per_block, k, k)
    eye = jnp.eye(per_block, dtype=w.dtype)
    out = jnp.einsum('bpij,pq->bpiqj', w, eye)
    return out.reshape(nb, per_block * k, per_block * k)


def _pair_tables():
    lo, hi = [], []
    for g in range(N_GROUPS):
        for a in range(EXP_PER_GROUP):
            for b in range(a + 1, EXP_PER_GROUP):
                lo.append(g * EXP_PER_GROUP + a)
                hi.append(g * EXP_PER_GROUP + b)
    return jnp.array(lo, jnp.int32), jnp.array(hi, jnp.int32)


def kernel(x, norm1_g, w_in, pool_w, pool_scale, conv_w, conv_b, rg_w_r, rg_b_r, rg_w_i, rg_b_i, rg_lambda, proj_a, proj_b, w_out, norm2_g, router_group_w, router_group_b, router_expert_w, router_expert_b, exp_w_gate, exp_w_up, exp_w_down, norm_f_g):
    bsz, seq, d = x.shape
    t = bsz * seq
    assert w_in.shape[0] == 1, "single-layer block"
    l = 0
    row = lambda v: v.reshape(1, -1).astype(F32)
    heads_per_blk = MXU_DIM // (LRU_WIDTH // rg_w_r.shape[1])
    groups_per_blk = MXU_DIM // POOL_GROUP_DIM
    n_rt = N_GROUPS + N_EXP
    wrt = jnp.concatenate([router_group_w[l].T, router_expert_w[l].T,
                           jnp.zeros((RT_ROWS - n_rt, d), F32)], axis=0)
    brt = jnp.concatenate([router_group_b[l], router_expert_b[l], jnp.zeros((RT_ROWS - n_rt,), F32)])
    hs = _mixer_tiles(bsz, seq)[0] // POST_SPLIT
    brt = jnp.broadcast_to(brt[:, None].astype(F32), (RT_ROWS, hs))

    w_in_b = w_in[l].astype(BF16)
    w_slab = jnp.concatenate([w_in_b[:, O_GATE:O_GATE + LRU_WIDTH], w_in_b[:, O_GB:O_GB + D_MODEL]], axis=1)
    w_slab = w_slab.reshape(d, N_SLABS, MXU_DIM).transpose(1, 0, 2)

    x1, h2e, cnt = _mixer(
        x, row(norm1_g[l]), jnp.concatenate([w_in_b[:, 0:O_GATE], w_in_b[:, O_GA:O_GB]], axis=1),
        _block_diag(pool_w[l], groups_per_blk).astype(BF16),
        row(pool_scale[l]), conv_w[l].astype(F32), row(conv_b[l]),
        _block_diag(rg_w_r[l], heads_per_blk).astype(BF16), _block_diag(rg_w_i[l], heads_per_blk).astype(BF16),
        row(rg_b_r[l]), row(rg_b_i[l]), row(rg_lambda[l]),
        proj_a[l].astype(BF16), proj_b[l].astype(BF16), w_out[l].astype(BF16), row(norm2_g[l]),
        wrt.astype(BF16), brt, w_slab)

    tm = min(EXPERT_TILE, t)
    n_tiles_max = -(-(t + N_CLASSES * (tm - 1)) // tm)
    cls = h2e[:, D_MODEL + 2].astype(jnp.int32)
    rank = h2e[:, D_MODEL + 3].astype(jnp.int32)
    counts = cnt[0:N_CLASSES, 0].astype(jnp.int32)
    padded = ((counts + tm - 1) // tm) * tm
    ends = jnp.cumsum(padded)
    offs = ends - padded
    pos = rank + jnp.sum(jnp.where(cls[:, None] == jnp.arange(N_CLASSES)[None, :], offs[None, :], 0), axis=1)
    n_tiles = ends[-1] // tm
    tile_cls = jnp.minimum(
        jnp.sum((jnp.arange(n_tiles_max)[:, None] * tm >= ends[None, :]).astype(jnp.int32), axis=1),
        N_CLASSES - 1)
    pair_lo, pair_hi = _pair_tables()
    meta = jnp.concatenate([n_tiles[None], pair_lo[tile_cls], pair_hi[tile_cls]]).astype(jnp.int32)

    ztile = jnp.concatenate([jnp.where(padded > 0, ends - tm, -1), n_tiles[None]]).astype(jnp.int32)
    xs = _dispatch(pos, ztile, h2e, n_tiles_max * tm, tm)
    ys = _experts(meta, xs, exp_w_gate[l].astype(BF16), exp_w_up[l].astype(BF16),
                  exp_w_down[l].astype(BF16), n_tiles_max)
    return _combine(pos, x1, row(norm_f_g), ys, bsz, seq)
```

```python
import jax
import jax.numpy as jnp
from jax import lax
from jax.experimental import pallas as pl
from jax.experimental.pallas import tpu as pltpu

F32 = jnp.float32
BF16 = jnp.bfloat16

LANES = 128
SUBLANES = 8
MXU_DIM = 256
NB = SUBLANES

D_MODEL = 1024
POOL_WIDTH = 512
POOL_GROUP_DIM = 128
POOL_WINDOWS = (2, 4, 8, 16)
POOL_HALO = max(POOL_WINDOWS) * NB
LRU_WIDTH = 1024
CONV_WIDTH = 4
CONV_HALO = (CONV_WIDTH - 1) * NB
RG_C = 8.0
N_GROUPS = 4
EXP_PER_GROUP = 4
N_EXP = 16
D_EXPERT = 512
EPS = 1e-6
N_PAIRS = 6
N_CLASSES = N_GROUPS * N_PAIRS

META_W = LANES
ROW_W = D_MODEL + META_W
RT_ROWS = 32

O_POOL = 0
O_LRU = POOL_WIDTH
O_GATE = O_LRU + LRU_WIDTH
O_GA = O_GATE + LRU_WIDTH
O_GB = O_GA + D_MODEL
IN_COLS = O_GB + D_MODEL
O_GA_MAIN = O_GATE

SEQ_TILE = 512
POST_SPLIT = 1
N_SLABS = 2 * LRU_WIDTH // MXU_DIM
TOKEN_TILE = 512
EXPERT_TILE = 512
VMEM_LIMIT = 56 * 1024 * 1024


def _sigmoid(v):
    return 0.5 * jnp.tanh(0.5 * v) + 0.5


def _gelu_tanh(v):
    c = 0.7978845608028654
    return 0.5 * v * (1.0 + jnp.tanh(c * (v + 0.044715 * (v * v * v))))


def _rms_norm(v, g):
    ms = jnp.mean(v * v, axis=-1, keepdims=True)
    return v * lax.rsqrt(ms + EPS) * g


def _dot(a, b):
    return jnp.dot(a, b, preferred_element_type=F32)


def _dot_nt(a, b):
    return lax.dot_general(a, b, (((1,), (1,)), ((), ())), preferred_element_type=F32)


def _first_index_of_max(v, idx, sentinel):
    m = jnp.max(v, axis=0, keepdims=True)
    return m, jnp.min(jnp.where(v == m, idx, sentinel), axis=0, keepdims=True)


def _route(lt, counts):
    n = lt.shape[1]
    sub = lax.broadcasted_iota(jnp.int32, (EXP_PER_GROUP, n), 0).astype(F32)
    none = jnp.float32(EXP_PER_GROUP)
    neg = jnp.float32(-3.0e38)
    gl = lt[0:N_GROUPS, :]
    gmax, gidx = _first_index_of_max(gl, sub, none)
    p_top = 1.0 / jnp.sum(jnp.exp(gl - gmax), axis=0, keepdims=True)
    el = lt[N_GROUPS + (N_GROUPS - 1) * EXP_PER_GROUP:N_GROUPS + N_GROUPS * EXP_PER_GROUP, :]
    for g in range(N_GROUPS - 2, -1, -1):
        lo = N_GROUPS + g * EXP_PER_GROUP
        el = jnp.where(gidx == float(g), lt[lo:lo + EXP_PER_GROUP, :], el)
    v1, i1 = _first_index_of_max(el, sub, none)
    v2, i2 = _first_index_of_max(jnp.where(sub == i1, neg, el), sub, none)
    ex = jnp.exp(v2 - v1)
    w1 = p_top / (1.0 + ex)
    w2 = p_top * ex / (1.0 + ex)
    first_lo = i1 < i2
    w_lo = jnp.where(first_lo, w1, w2)
    w_hi = jnp.where(first_lo, w2, w1)
    pa_ = jnp.minimum(i1, i2)
    pb_ = jnp.maximum(i1, i2)
    cls = gidx * N_PAIRS + pa_ * (7.0 - pa_) * 0.5 + (pb_ - pa_ - 1.0)

    crow = lax.broadcasted_iota(jnp.int32, (RT_ROWS, n), 0).astype(F32)
    onehot = crow == cls
    t_row = lax.broadcasted_iota(jnp.int32, (n, n), 0)
    t_col = lax.broadcasted_iota(jnp.int32, (n, n), 1)
    earlier = jnp.where(t_row < t_col, 1.0, 0.0).astype(BF16)
    prior = _dot(jnp.where(onehot, 1.0, 0.0).astype(BF16), earlier) + counts
    rank = jnp.sum(jnp.where(onehot, prior, 0.0), axis=0, keepdims=True)
    counts = counts + jnp.sum(jnp.where(onehot, 1.0, 0.0), axis=1, keepdims=True)

    mrow = lax.broadcasted_iota(jnp.int32, (SUBLANES, n), 0)
    meta = jnp.where(mrow == 0, w_lo,
                     jnp.where(mrow == 1, w_hi,
                               jnp.where(mrow == 2, cls,
                                         jnp.where(mrow == 3, rank, 0.0))))
    return meta, counts


def _mixer_kernel(x_hbm, g1_ref, w_in_ref, pool_bd_ref, pool_scale_ref, conv_w_ref, conv_b_ref,
                  wr_ref, wi_ref, br_ref, bi_ref, lam_ref, pa_ref, pb_ref, wo_ref, g2_ref,
                  wrt_ref, brt_ref, w_slab_ref,
                  x1_ref, h2e_ref, cnt_ref,
                  xbuf, xsem, pool_ext, lru_ext, a_buf, b_buf, mixa_buf, h_buf, zg_buf, h_carry, cnt_scr):
    g = pl.program_id(0)
    c = pl.program_id(1)
    nc = pl.num_programs(1)
    tq = xbuf.shape[1]
    ts = tq * NB
    step = g * nc + c
    slot = step % 2

    def x_copies(g_, c_, slot_):
        return [pltpu.make_async_copy(x_hbm.at[g_ * NB + bb, pl.ds(c_ * tq, tq), :],
                                      xbuf.at[slot_, :, bb, :], xsem.at[slot_]) for bb in range(NB)]

    @pl.when(step == 0)
    def _():
        for cp in x_copies(g, c, slot):
            cp.start()

    @pl.when(step + 1 < pl.num_programs(0) * nc)
    def _():
        wrap = c + 1 == nc
        for cp in x_copies(jnp.where(wrap, g + 1, g), jnp.where(wrap, 0, c + 1), 1 - slot):
            cp.start()

    @pl.when(c == 0)
    def _():
        pool_ext[0:POOL_HALO, :] = jnp.zeros((POOL_HALO, POOL_WIDTH), F32)
        lru_ext[0:CONV_HALO, :] = jnp.zeros((CONV_HALO, LRU_WIDTH), F32)
        h_carry[...] = jnp.zeros_like(h_carry)

    @pl.when(step == 0)
    def _():
        cnt_scr[...] = jnp.zeros_like(cnt_scr)

    for cp in x_copies(g, c, slot):
        cp.wait()
    x = xbuf[slot].reshape(ts, D_MODEL)
    h = _rms_norm(x, g1_ref[...]).astype(BF16)
    h_buf[...] = h

    pool_ext[POOL_HALO:POOL_HALO + ts, :] = _dot(h, w_in_ref[:, O_POOL:O_POOL + POOL_WIDTH])
    ext = pool_ext[...]
    pool_ext[0:POOL_HALO, :] = ext[ts:ts + POOL_HALO, :]
    t_idx = lax.broadcasted_iota(jnp.int32, (ts, POOL_GROUP_DIM), 0) // NB
    pos1 = (c * tq + 1 + t_idx).astype(F32)
    zs = []
    for gi, w in enumerate(POOL_WINDOWS):
        s = ext[:, gi * POOL_GROUP_DIM:(gi + 1) * POOL_GROUP_DIM]
        u = s[POOL_HALO:]
        d = 1
        while d < w:
            s = s[d * NB:] + s[:-d * NB]
            d *= 2
        off = POOL_HALO - (w - 1) * NB
        win = s[off:off + ts]
        zs.append(win / jnp.minimum(pos1, float(w)) - u)
    z = jnp.concatenate(zs, axis=1).astype(BF16)
    y_pool = jnp.concatenate(
        [_dot(z[:, j * MXU_DIM:(j + 1) * MXU_DIM], pool_bd_ref[j]) for j in range(POOL_WIDTH // MXU_DIM)],
        axis=1) * pool_scale_ref[...]
    y_a = _dot(y_pool.astype(BF16), pa_ref[...])
    mixa_buf[...] = _sigmoid(_dot(h, w_in_ref[:, O_GA_MAIN:O_GA_MAIN + D_MODEL])) * y_a

    lru_ext[CONV_HALO:CONV_HALO + ts, :] = _dot(h, w_in_ref[:, O_LRU:O_LRU + LRU_WIDTH])
    xc = conv_b_ref[...] + conv_w_ref[CONV_WIDTH - 1:CONV_WIDTH, :] * lru_ext[CONV_HALO:CONV_HALO + ts, :]
    for k in range(CONV_WIDTH - 1):
        o = k * NB
        xc = xc + conv_w_ref[k:k + 1, :] * lru_ext[o:o + ts, :]
    lru_ext[0:CONV_HALO, :] = lru_ext[ts:ts + CONV_HALO, :]
    xcb = xc.astype(BF16)
    nblk = LRU_WIDTH // MXU_DIM
    r = _sigmoid(jnp.concatenate(
        [_dot(xcb[:, j * MXU_DIM:(j + 1) * MXU_DIM], wr_ref[j]) for j in range(nblk)], axis=1) + br_ref[...])
    i = _sigmoid(jnp.concatenate(
        [_dot(xcb[:, j * MXU_DIM:(j + 1) * MXU_DIM], wi_ref[j]) for j in range(nblk)], axis=1) + bi_ref[...])
    nlam = -lam_ref[...]
    softplus = jnp.maximum(nlam, 0.0) + jnp.log1p(jnp.exp(-jnp.abs(nlam)))
    log_a = (-RG_C * softplus) * r
    a = jnp.exp(log_a)
    a_buf[...] = a
    v = jnp.tanh(-log_a) * (1.0 + a * a)
    b_buf[...] = jnp.where(v > 0.0, v * lax.rsqrt(v), 0.0) * (i * xc)

    steps_per_trip = tq // N_SLABS
    carry = h_carry[...]
    for j in range(N_SLABS):
        zg_buf[j] = _dot(h_buf[...], w_slab_ref[j])
        for q in range(steps_per_trip):
            r0 = (j * steps_per_trip + q) * NB
            carry = b_buf[r0:r0 + NB, :] + a_buf[r0:r0 + NB, :] * carry
            b_buf[r0:r0 + NB, :] = carry
    h_carry[...] = carry

    hs = ts // POST_SPLIT
    counts = cnt_scr[...]
    for part in range(POST_SPLIT):
        rows = slice(part * hs, (part + 1) * hs)
        half = N_SLABS // 2
        z_gate = jnp.concatenate([zg_buf[j, rows, :] for j in range(half)], axis=1)
        z_gb = jnp.concatenate([zg_buf[half + j, rows, :] for j in range(half)], axis=1)
        y_lru = b_buf[rows, :] * _gelu_tanh(z_gate)
        y_b = _dot(y_lru.astype(BF16), pb_ref[...])
        mix = mixa_buf[rows, :] + _sigmoid(z_gb) * y_b
        x1 = x[rows, :] + _dot(mix.astype(BF16), wo_ref[...])
        x1_ref[rows, :] = x1
        h2 = _rms_norm(x1, g2_ref[...])
        h2e_ref[rows, 0:D_MODEL] = h2
        lt = _dot_nt(wrt_ref[...], h2.astype(BF16)) + brt_ref[...]
        meta, counts = _route(lt, counts)
        meta_t = jnp.concatenate([meta, jnp.zeros((META_W - SUBLANES, hs), F32)], axis=0).T
        h2e_ref[rows, D_MODEL:ROW_W] = meta_t
    cnt_scr[...] = counts
    cnt_ref[...] = counts


def _const_spec(shape):
    zeros = (0,) * len(shape)
    return pl.BlockSpec(shape, lambda b, c: zeros, pipeline_mode=pl.Buffered(1))


def _mixer_tiles(bsz, seq):
    tq = min(SEQ_TILE // NB, seq)
    ts = tq * NB
    assert bsz % NB == 0 and seq % tq == 0 and tq % N_SLABS == 0 and ts >= POOL_HALO
    return ts, tq


def _mixer(x, g1, w_in, pool_bd, pool_scale, conv_w, conv_b, wr4, wi4, b_r, b_i, lam, pa, pb, wo, g2,
           wrt, brt, w_slab):
    bsz, seq, d = x.shape
    ts, tq = _mixer_tiles(bsz, seq)
    hs = ts // POST_SPLIT
    assert d == D_MODEL and hs % LANES == 0 and brt.shape == (RT_ROWS, hs)
    nc = seq // tq
    consts = (g1, w_in, pool_bd, pool_scale, conv_w, conv_b, wr4, wi4, b_r, b_i, lam, pa, pb, wo, g2,
              wrt, brt, w_slab)
    return pl.pallas_call(
        _mixer_kernel,
        grid=(bsz // NB, nc),
        in_specs=[pl.BlockSpec(memory_space=pl.ANY)] + [_const_spec(a.shape) for a in consts],
        out_specs=[
            pl.BlockSpec((ts, d), lambda g, c: (g * nc + c, 0)),
            pl.BlockSpec((ts, ROW_W), lambda g, c: (g * nc + c, 0)),
            pl.BlockSpec((RT_ROWS, hs), lambda g, c: (0, 0)),
        ],
        out_shape=[
            jax.ShapeDtypeStruct((bsz * seq, d), F32),
            jax.ShapeDtypeStruct((bsz * seq, ROW_W), F32),
            jax.ShapeDtypeStruct((RT_ROWS, hs), F32),
        ],
        scratch_shapes=[
            pltpu.VMEM((2, tq, NB, d), F32),
            pltpu.SemaphoreType.DMA((2,)),
            pltpu.VMEM((POOL_HALO + ts, POOL_WIDTH), F32),
            pltpu.VMEM((CONV_HALO + ts, LRU_WIDTH), F32),
            pltpu.VMEM((ts, LRU_WIDTH), F32),
            pltpu.VMEM((ts, LRU_WIDTH), F32),
            pltpu.VMEM((ts, D_MODEL), F32),
            pltpu.VMEM((ts, D_MODEL), BF16),
            pltpu.VMEM((N_SLABS, ts, MXU_DIM), F32),
            pltpu.VMEM((NB, LRU_WIDTH), F32),
            pltpu.VMEM((RT_ROWS, hs), F32),
        ],
        compiler_params=pltpu.CompilerParams(
            dimension_semantics=("arbitrary", "arbitrary"), vmem_limit_bytes=VMEM_LIMIT),
        name="mixer",
    )(x, *consts)


def _dispatch_kernel(pos_ref, ztile_ref, h2e_ref, xs_ref, zbuf, sem, zsem):
    tb = h2e_ref.shape[0]
    tm = zbuf.shape[0]
    base = pl.program_id(0) * tb

    @pl.when(pl.program_id(0) == 0)
    def _():
        zbuf[...] = jnp.zeros_like(zbuf)

        def zero_copy(c):
            start = pl.multiple_of(ztile_ref[c], SUBLANES)
            return pltpu.make_async_copy(zbuf, xs_ref.at[pl.ds(start, tm)], zsem)

        for c in range(N_CLASSES):
            @pl.when(ztile_ref[c] >= 0)
            def _():
                zero_copy(c).start()
        for c in range(N_CLASSES):
            @pl.when(ztile_ref[c] >= 0)
            def _():
                zero_copy(c).wait()

        def tail_copy(k):
            return pltpu.make_async_copy(zbuf, xs_ref.at[pl.ds(pl.multiple_of(k * tm, SUBLANES), tm)], zsem)

        n_used = ztile_ref[N_CLASSES]
        n_all = xs_ref.shape[0] // tm
        lax.fori_loop(n_used, n_all, lambda k, c: (tail_copy(k).start(), c)[1], 0)
        lax.fori_loop(n_used, n_all, lambda k, c: (tail_copy(k).wait(), c)[1], 0)

    for r in range(tb):
        pltpu.make_async_copy(
            h2e_ref.at[pl.ds(r, 1)], xs_ref.at[pl.ds(pos_ref[base + r], 1)], sem).start(priority=r % 2)
    pltpu.make_async_copy(h2e_ref, xs_ref.at[pl.ds(0, tb)], sem).wait()


def _dispatch(pos, ztile, h2e, n_slots, tm):
    t = h2e.shape[0]
    tb = min(TOKEN_TILE, t)
    assert t % tb == 0 and n_slots >= tb
    return pl.pallas_call(
        _dispatch_kernel,
        grid_spec=pltpu.PrefetchScalarGridSpec(
            num_scalar_prefetch=2,
            grid=(t // tb,),
            in_specs=[pl.BlockSpec((tb, ROW_W), lambda i, pos, zt: (i, 0))],
            out_specs=pl.BlockSpec(memory_space=pl.ANY),
            scratch_shapes=[pltpu.VMEM((tm, ROW_W), F32),
                            pltpu.SemaphoreType.DMA(()), pltpu.SemaphoreType.DMA(())],
        ),
        out_shape=jax.ShapeDtypeStruct((n_slots, ROW_W), F32),
        compiler_params=pltpu.CompilerParams(
            dimension_semantics=("arbitrary",), vmem_limit_bytes=VMEM_LIMIT),
        name="dispatch",
    )(pos, ztile, h2e)


def _expert_kernel(meta_ref, xs_ref, wg_lo, wu_lo, wd_lo, wg_hi, wu_hi, wd_hi, ys_ref):
    i = pl.program_id(0)

    @pl.when(i < meta_ref[0])
    def _():
        x = xs_ref[:, 0:D_MODEL].astype(BF16)
        w_lo = xs_ref[:, D_MODEL:D_MODEL + 1]
        w_hi = xs_ref[:, D_MODEL + 1:D_MODEL + 2]

        def ffn(wg, wu, wd, w):
            g = _dot(x, wg[0])
            u = _dot(x, wu[0])
            act = (g * _sigmoid(g)) * u * w
            return _dot(act.astype(BF16), wd[0])

        ys_ref[...] = ffn(wg_lo, wu_lo, wd_lo, w_lo) + ffn(wg_hi, wu_hi, wd_hi, w_hi)

    @pl.when(i >= meta_ref[0])
    def _():
        ys_ref[...] = jnp.zeros_like(ys_ref)


def _experts(meta, xs, e_gate, e_up, e_down, n_tiles_max):
    tm = xs.shape[0] // n_tiles_max

    def row_map(i, m):
        return (jnp.minimum(i, m[0] - 1), 0)

    def lo_map(i, m):
        return (m[1 + jnp.minimum(i, m[0] - 1)], 0, 0)

    def hi_map(i, m):
        return (m[1 + n_tiles_max + jnp.minimum(i, m[0] - 1)], 0, 0)

    up_spec = lambda mp: pl.BlockSpec((1, D_MODEL, D_EXPERT), mp)
    down_spec = lambda mp: pl.BlockSpec((1, D_EXPERT, D_MODEL), mp)
    return pl.pallas_call(
        _expert_kernel,
        grid_spec=pltpu.PrefetchScalarGridSpec(
            num_scalar_prefetch=1,
            grid=(n_tiles_max,),
            in_specs=[pl.BlockSpec((tm, ROW_W), row_map),
                      up_spec(lo_map), up_spec(lo_map), down_spec(lo_map),
                      up_spec(hi_map), up_spec(hi_map), down_spec(hi_map)],
            out_specs=pl.BlockSpec((tm, D_MODEL), lambda i, m: (i, 0)),
        ),
        out_shape=jax.ShapeDtypeStruct((xs.shape[0], D_MODEL), F32),
        compiler_params=pltpu.CompilerParams(
            dimension_semantics=("arbitrary",), vmem_limit_bytes=VMEM_LIMIT),
        name="experts",
    )(meta, xs, e_gate, e_up, e_down, e_gate, e_up, e_down)


def _combine_kernel(pos_ref, x1_ref, gf_ref, ys_ref, out_hbm, ybuf, sems, obuf, osems):
    tb = x1_ref.shape[0]
    tq = obuf.shape[1]
    nc = out_hbm.shape[1] // tq
    i = pl.program_id(0)
    slot = i % 2

    def out_copies(step, src_slot):
        g_, c_ = step // nc, step % nc
        return [pltpu.make_async_copy(obuf.at[src_slot, :, bb, :],
                                      out_hbm.at[g_ * NB + bb, pl.ds(c_ * tq, tq), :],
                                      osems.at[src_slot]) for bb in range(NB)]

    def gather(step, dst_slot):
        base = step * tb
        for r in range(tb):
            pltpu.make_async_copy(ys_ref.at[pl.ds(pos_ref[base + r], 1)],
                                  ybuf.at[dst_slot, pl.ds(r, 1)], sems.at[dst_slot]).start(priority=r % 2)

    @pl.when(i == 0)
    def _():
        gather(0, 0)

    @pl.when(i + 1 < pl.num_programs(0))
    def _():
        gather(i + 1, 1 - slot)

    pltpu.make_async_copy(ys_ref.at[pl.ds(0, tb)], ybuf.at[slot], sems.at[slot]).wait()

    @pl.when(i >= 2)
    def _():
        for cp in out_copies(i - 2, slot):
            cp.wait()

    obuf[slot] = _rms_norm(x1_ref[...] + ybuf[slot], gf_ref[...]).reshape(tq, NB, D_MODEL)
    for cp in out_copies(i, slot):
        cp.start()

    @pl.when(i + 1 == pl.num_programs(0))
    def _():
        @pl.when(i >= 1)
        def _():
            for cp in out_copies(i - 1, 1 - slot):
                cp.wait()
        for cp in out_copies(i, slot):
            cp.wait()


def _combine(pos, x1, gf, ys, bsz, seq):
    t = x1.shape[0]
    tb, tq = _mixer_tiles(bsz, seq)
    assert t % tb == 0
    return pl.pallas_call(
        _combine_kernel,
        grid_spec=pltpu.PrefetchScalarGridSpec(
            num_scalar_prefetch=1,
            grid=(t // tb,),
            in_specs=[pl.BlockSpec((tb, D_MODEL), lambda i, pos: (i, 0)),
                      pl.BlockSpec((1, D_MODEL), lambda i, pos: (0, 0)),
                      pl.BlockSpec(memory_space=pl.ANY)],
            out_specs=pl.BlockSpec(memory_space=pl.ANY),
            scratch_shapes=[pltpu.VMEM((2, tb, D_MODEL), F32), pltpu.SemaphoreType.DMA((2,)),
                            pltpu.VMEM((2, tq, NB, D_MODEL), F32), pltpu.SemaphoreType.DMA((2,))],
        ),
        out_shape=jax.ShapeDtypeStruct((bsz, seq, D_MODEL), F32),
        compiler_params=pltpu.CompilerParams(
            dimension_semantics=("arbitrary",), vmem_limit_bytes=VMEM_LIMIT),
        name="combine",
    )(pos, x1, gf, ys)


def _block_diag(w, per_block):
    n, k, _ = w.shape
    nb = n // per_block
    w = w.reshape(nb, per_block, k, k)
    eye = jnp.eye(per_block, dtype=w.dtype)
    out = jnp.einsum('bpij,pq->bpiqj', w, eye)
    return out.reshape(nb, per_block * k, per_block * k)


def _pair_tables():
    lo, hi = [], []
    for g in range(N_GROUPS):
        for a in range(EXP_PER_GROUP):
            for b in range(a + 1, EXP_PER_GROUP):
                lo.append(g * EXP_PER_GROUP + a)
                hi.append(g * EXP_PER_GROUP + b)
    return jnp.array(lo, jnp.int32), jnp.array(hi, jnp.int32)


def kernel(x, norm1_g, w_in, pool_w, pool_scale, conv_w, conv_b, rg_w_r, rg_b_r, rg_w_i, rg_b_i, rg_lambda, proj_a, proj_b, w_out, norm2_g, router_group_w, router_group_b, router_expert_w, router_expert_b, exp_w_gate, exp_w_up, exp_w_down, norm_f_g):
    bsz, seq, d = x.shape
    t = bsz * seq
    assert w_in.shape[0] == 1, "single-layer block"
    l = 0
    row = lambda v: v.reshape(1, -1).astype(F32)
    heads_per_blk = MXU_DIM // (LRU_WIDTH // rg_w_r.shape[1])
    groups_per_blk = MXU_DIM // POOL_GROUP_DIM
    n_rt = N_GROUPS + N_EXP
    wrt = jnp.concatenate([router_group_w[l].T, router_expert_w[l].T,
                           jnp.zeros((RT_ROWS - n_rt, d), F32)], axis=0)
    brt = jnp.concatenate([router_group_b[l], router_expert_b[l], jnp.zeros((RT_ROWS - n_rt,), F32)])
    hs = _mixer_tiles(bsz, seq)[0] // POST_SPLIT
    brt = jnp.broadcast_to(brt[:, None].astype(F32), (RT_ROWS, hs))

    w_in_b = w_in[l].astype(BF16)
    w_slab = jnp.concatenate([w_in_b[:, O_GATE:O_GATE + LRU_WIDTH], w_in_b[:, O_GB:O_GB + D_MODEL]], axis=1)
    w_slab = w_slab.reshape(d, N_SLABS, MXU_DIM).transpose(1, 0, 2)

    x1, h2e, cnt = _mixer(
        x, row(norm1_g[l]), jnp.concatenate([w_in_b[:, 0:O_GATE], w_in_b[:, O_GA:O_GB]], axis=1),
        _block_diag(pool_w[l], groups_per_blk).astype(BF16),
        row(pool_scale[l]), conv_w[l].astype(F32), row(conv_b[l]),
        _block_diag(rg_w_r[l], heads_per_blk).astype(BF16), _block_diag(rg_w_i[l], heads_per_blk).astype(BF16),
        row(rg_b_r[l]), row(rg_b_i[l]), row(rg_lambda[l]),
        proj_a[l].astype(BF16), proj_b[l].astype(BF16), w_out[l].astype(BF16), row(norm2_g[l]),
        wrt.astype(BF16), brt, w_slab)

    tm = min(EXPERT_TILE, t)
    n_tiles_max = -(-(t + N_CLASSES * (tm - 1)) // tm)
    cls = h2e[:, D_MODEL + 2].astype(jnp.int32)
    rank = h2e[:, D_MODEL + 3].astype(jnp.int32)
    counts = cnt[0:N_CLASSES, 0].astype(jnp.int32)
    padded = ((counts + tm - 1) // tm) * tm
    ends = jnp.cumsum(padded)
    offs = ends - padded
    pos = rank
    for k in range(N_CLASSES):
        pos = pos + jnp.where(cls == k, offs[k], 0)
    n_tiles = ends[-1] // tm
    tile_cls = jnp.minimum(
        jnp.sum((jnp.arange(n_tiles_max)[:, None] * tm >= ends[None, :]).astype(jnp.int32), axis=1),
        N_CLASSES - 1)
    pair_lo, pair_hi = _pair_tables()
    meta = jnp.concatenate([n_tiles[None], pair_lo[tile_cls], pair_hi[tile_cls]]).astype(jnp.int32)

    ztile = jnp.concatenate([jnp.where(padded > 0, ends - tm, -1), n_tiles[None]]).astype(jnp.int32)
    xs = _dispatch(pos, ztile, h2e, n_tiles_max * tm, tm)
    ys = _experts(meta, xs, exp_w_gate[l].astype(BF16), exp_w_up[l].astype(BF16),
                  exp_w_down[l].astype(BF16), n_tiles_max)
    return _combine(pos, x1, row(norm_f_g), ys, bsz, seq)
```

```python
import jax
import jax.numpy as jnp
from jax import lax
from jax.experimental import pallas as pl
from jax.experimental.pallas import tpu as pltpu

F32 = jnp.float32
BF16 = jnp.bfloat16

LANES = 128
SUBLANES = 8
MXU_DIM = 256
NB = SUBLANES

D_MODEL = 1024
POOL_WIDTH = 512
POOL_GROUP_DIM = 128
POOL_WINDOWS = (2, 4, 8, 16)
POOL_HALO = max(POOL_WINDOWS) * NB
LRU_WIDTH = 1024
CONV_WIDTH = 4
CONV_HALO = (CONV_WIDTH - 1) * NB
RG_C = 8.0
N_GROUPS = 4
EXP_PER_GROUP = 4
N_EXP = 16
D_EXPERT = 512
EPS = 1e-6
N_PAIRS = 6
N_CLASSES = N_GROUPS * N_PAIRS

META_W = LANES
ROW_W = D_MODEL + META_W
RT_ROWS = 32

O_POOL = 0
O_LRU = POOL_WIDTH
O_GATE = O_LRU + LRU_WIDTH
O_GA = O_GATE + LRU_WIDTH
O_GB = O_GA + D_MODEL
IN_COLS = O_GB + D_MODEL
O_GA_MAIN = O_GATE

SEQ_TILE = 512
POST_SPLIT = 1
N_SLABS = 2 * LRU_WIDTH // MXU_DIM
DISPATCH_TILE = 1024
EXPERT_TILE = 512
VMEM_LIMIT = 56 * 1024 * 1024


def _sigmoid(v):
    return 0.5 * jnp.tanh(0.5 * v) + 0.5


def _gelu_tanh(v):
    c = 0.7978845608028654
    return 0.5 * v * (1.0 + jnp.tanh(c * (v + 0.044715 * (v * v * v))))


def _rms_norm(v, g):
    ms = jnp.mean(v * v, axis=-1, keepdims=True)
    return v * lax.rsqrt(ms + EPS) * g


def _dot(a, b):
    return jnp.dot(a, b, preferred_element_type=F32)


def _dot_nt(a, b):
    return lax.dot_general(a, b, (((1,), (1,)), ((), ())), preferred_element_type=F32)


def _first_index_of_max(v, idx, sentinel):
    m = jnp.max(v, axis=0, keepdims=True)
    return m, jnp.min(jnp.where(v == m, idx, sentinel), axis=0, keepdims=True)


def _route(lt, counts):
    n = lt.shape[1]
    sub = lax.broadcasted_iota(jnp.int32, (EXP_PER_GROUP, n), 0).astype(F32)
    none = jnp.float32(EXP_PER_GROUP)
    neg = jnp.float32(-3.0e38)
    gl = lt[0:N_GROUPS, :]
    gmax, gidx = _first_index_of_max(gl, sub, none)
    p_top = 1.0 / jnp.sum(jnp.exp(gl - gmax), axis=0, keepdims=True)
    el = lt[N_GROUPS + (N_GROUPS - 1) * EXP_PER_GROUP:N_GROUPS + N_GROUPS * EXP_PER_GROUP, :]
    for g in range(N_GROUPS - 2, -1, -1):
        lo = N_GROUPS + g * EXP_PER_GROUP
        el = jnp.where(gidx == float(g), lt[lo:lo + EXP_PER_GROUP, :], el)
    v1, i1 = _first_index_of_max(el, sub, none)
    v2, i2 = _first_index_of_max(jnp.where(sub == i1, neg, el), sub, none)
    ex = jnp.exp(v2 - v1)
    w1 = p_top / (1.0 + ex)
    w2 = p_top * ex / (1.0 + ex)
    first_lo = i1 < i2
    w_lo = jnp.where(first_lo, w1, w2)
    w_hi = jnp.where(first_lo, w2, w1)
    pa_ = jnp.minimum(i1, i2)
    pb_ = jnp.maximum(i1, i2)
    cls = gidx * N_PAIRS + pa_ * (7.0 - pa_) * 0.5 + (pb_ - pa_ - 1.0)

    crow = lax.broadcasted_iota(jnp.int32, (RT_ROWS, n), 0).astype(F32)
    onehot = crow == cls
    t_row = lax.broadcasted_iota(jnp.int32, (n, n), 0)
    t_col = lax.broadcasted_iota(jnp.int32, (n, n), 1)
    earlier = jnp.where(t_row < t_col, 1.0, 0.0).astype(BF16)
    prior = _dot(jnp.where(onehot, 1.0, 0.0).astype(BF16), earlier) + counts
    rank = jnp.sum(jnp.where(onehot, prior, 0.0), axis=0, keepdims=True)
    counts = counts + jnp.sum(jnp.where(onehot, 1.0, 0.0), axis=1, keepdims=True)

    mrow = lax.broadcasted_iota(jnp.int32, (SUBLANES, n), 0)
    meta = jnp.where(mrow == 0, w_lo,
                     jnp.where(mrow == 1, w_hi,
                               jnp.where(mrow == 2, cls,
                                         jnp.where(mrow == 3, rank, 0.0))))
    return meta, counts


def _mixer_kernel(x_hbm, g1_ref, w_in_ref, pool_bd_ref, pool_scale_ref, conv_w_ref, conv_b_ref,
                  wr_ref, wi_ref, br_ref, bi_ref, lam_ref, pa_ref, pb_ref, wo_ref, g2_ref,
                  wrt_ref, brt_ref, w_slab_ref,
                  x1_ref, h2e_ref, cnt_ref, route_ref,
                  xbuf, xsem, pool_ext, lru_ext, a_buf, b_buf, mixa_buf, h_buf, zg_buf, h_carry, cnt_scr):
    g = pl.program_id(0)
    c = pl.program_id(1)
    nc = pl.num_programs(1)
    tq = xbuf.shape[1]
    ts = tq * NB
    step = g * nc + c
    slot = step % 2

    def x_copies(g_, c_, slot_):
        return [pltpu.make_async_copy(x_hbm.at[g_ * NB + bb, pl.ds(c_ * tq, tq), :],
                                      xbuf.at[slot_, :, bb, :], xsem.at[slot_]) for bb in range(NB)]

    @pl.when(step == 0)
    def _():
        for cp in x_copies(g, c, slot):
            cp.start()

    @pl.when(step + 1 < pl.num_programs(0) * nc)
    def _():
        wrap = c + 1 == nc
        for cp in x_copies(jnp.where(wrap, g + 1, g), jnp.where(wrap, 0, c + 1), 1 - slot):
            cp.start()

    @pl.when(c == 0)
    def _():
        pool_ext[0:POOL_HALO, :] = jnp.zeros((POOL_HALO, POOL_WIDTH), F32)
        lru_ext[0:CONV_HALO, :] = jnp.zeros((CONV_HALO, LRU_WIDTH), F32)
        h_carry[...] = jnp.zeros_like(h_carry)

    @pl.when(step == 0)
    def _():
        cnt_scr[...] = jnp.zeros_like(cnt_scr)

    for cp in x_copies(g, c, slot):
        cp.wait()
    x = xbuf[slot].reshape(ts, D_MODEL)
    h = _rms_norm(x, g1_ref[...]).astype(BF16)
    h_buf[...] = h

    pool_ext[POOL_HALO:POOL_HALO + ts, :] = _dot(h, w_in_ref[:, O_POOL:O_POOL + POOL_WIDTH])
    ext = pool_ext[...]
    pool_ext[0:POOL_HALO, :] = ext[ts:ts + POOL_HALO, :]
    t_idx = lax.broadcasted_iota(jnp.int32, (ts, POOL_GROUP_DIM), 0) // NB
    pos1 = (c * tq + 1 + t_idx).astype(F32)
    zs = []
    for gi, w in enumerate(POOL_WINDOWS):
        s = ext[:, gi * POOL_GROUP_DIM:(gi + 1) * POOL_GROUP_DIM]
        u = s[POOL_HALO:]
        d = 1
        while d < w:
            s = s[d * NB:] + s[:-d * NB]
            d *= 2
        off = POOL_HALO - (w - 1) * NB
        win = s[off:off + ts]
        zs.append(win / jnp.minimum(pos1, float(w)) - u)
    z = jnp.concatenate(zs, axis=1).astype(BF16)
    y_pool = jnp.concatenate(
        [_dot(z[:, j * MXU_DIM:(j + 1) * MXU_DIM], pool_bd_ref[j]) for j in range(POOL_WIDTH // MXU_DIM)],
        axis=1) * pool_scale_ref[...]
    y_a = _dot(y_pool.astype(BF16), pa_ref[...])
    mixa_buf[...] = _sigmoid(_dot(h, w_in_ref[:, O_GA_MAIN:O_GA_MAIN + D_MODEL])) * y_a

    lru_ext[CONV_HALO:CONV_HALO + ts, :] = _dot(h, w_in_ref[:, O_LRU:O_LRU + LRU_WIDTH])
    xc = conv_b_ref[...] + conv_w_ref[CONV_WIDTH - 1:CONV_WIDTH, :] * lru_ext[CONV_HALO:CONV_HALO + ts, :]
    for k in range(CONV_WIDTH - 1):
        o = k * NB
        xc = xc + conv_w_ref[k:k + 1, :] * lru_ext[o:o + ts, :]
    lru_ext[0:CONV_HALO, :] = lru_ext[ts:ts + CONV_HALO, :]
    xcb = xc.astype(BF16)
    nblk = LRU_WIDTH // MXU_DIM
    r = _sigmoid(jnp.concatenate(
        [_dot(xcb[:, j * MXU_DIM:(j + 1) * MXU_DIM], wr_ref[j]) for j in range(nblk)], axis=1) + br_ref[...])
    i = _sigmoid(jnp.concatenate(
        [_dot(xcb[:, j * MXU_DIM:(j + 1) * MXU_DIM], wi_ref[j]) for j in range(nblk)], axis=1) + bi_ref[...])
    nlam = -lam_ref[...]
    softplus = jnp.maximum(nlam, 0.0) + jnp.log1p(jnp.exp(-jnp.abs(nlam)))
    log_a = (-RG_C * softplus) * r
    a = jnp.exp(log_a)
    a_buf[...] = a
    v = jnp.tanh(-log_a) * (1.0 + a * a)
    b_buf[...] = jnp.where(v > 0.0, v * lax.rsqrt(v), 0.0) * (i * xc)

    steps_per_trip = tq // N_SLABS
    carry = h_carry[...]
    for j in range(N_SLABS):
        zg_buf[j] = _dot(h_buf[...], w_slab_ref[j])
        for q in range(steps_per_trip):
            r0 = (j * steps_per_trip + q) * NB
            carry = b_buf[r0:r0 + NB, :] + a_buf[r0:r0 + NB, :] * carry
            b_buf[r0:r0 + NB, :] = carry
    h_carry[...] = carry

    hs = ts // POST_SPLIT
    counts = cnt_scr[...]
    for part in range(POST_SPLIT):
        rows = slice(part * hs, (part + 1) * hs)
        half = N_SLABS // 2
        z_gate = jnp.concatenate([zg_buf[j, rows, :] for j in range(half)], axis=1)
        z_gb = jnp.concatenate([zg_buf[half + j, rows, :] for j in range(half)], axis=1)
        y_lru = b_buf[rows, :] * _gelu_tanh(z_gate)
        y_b = _dot(y_lru.astype(BF16), pb_ref[...])
        mix = mixa_buf[rows, :] + _sigmoid(z_gb) * y_b
        x1 = x[rows, :] + _dot(mix.astype(BF16), wo_ref[...])
        x1_ref[rows, :] = x1
        h2 = _rms_norm(x1, g2_ref[...])
        h2e_ref[rows, 0:D_MODEL] = h2
        lt = _dot_nt(wrt_ref[...], h2.astype(BF16)) + brt_ref[...]
        meta, counts = _route(lt, counts)
        route_ref[:, rows] = meta
        meta_t = jnp.concatenate([meta, jnp.zeros((META_W - SUBLANES, hs), F32)], axis=0).T
        h2e_ref[rows, D_MODEL:ROW_W] = meta_t
    cnt_scr[...] = counts
    cnt_ref[...] = counts


def _const_spec(shape):
    zeros = (0,) * len(shape)
    return pl.BlockSpec(shape, lambda b, c: zeros, pipeline_mode=pl.Buffered(1))


def _mixer_tiles(bsz, seq):
    tq = min(SEQ_TILE // NB, seq)
    ts = tq * NB
    assert bsz % NB == 0 and seq % tq == 0 and tq % N_SLABS == 0 and ts >= POOL_HALO
    return ts, tq


def _mixer(x, g1, w_in, pool_bd, pool_scale, conv_w, conv_b, wr4, wi4, b_r, b_i, lam, pa, pb, wo, g2,
           wrt, brt, w_slab):
    bsz, seq, d = x.shape
    ts, tq = _mixer_tiles(bsz, seq)
    hs = ts // POST_SPLIT
    assert d == D_MODEL and hs % LANES == 0 and brt.shape == (RT_ROWS, hs)
    nc = seq // tq
    consts = (g1, w_in, pool_bd, pool_scale, conv_w, conv_b, wr4, wi4, b_r, b_i, lam, pa, pb, wo, g2,
              wrt, brt, w_slab)
    return pl.pallas_call(
        _mixer_kernel,
        grid=(bsz // NB, nc),
        in_specs=[pl.BlockSpec(memory_space=pl.ANY)] + [_const_spec(a.shape) for a in consts],
        out_specs=[
            pl.BlockSpec((ts, d), lambda g, c: (g * nc + c, 0)),
            pl.BlockSpec((ts, ROW_W), lambda g, c: (g * nc + c, 0)),
            pl.BlockSpec((RT_ROWS, hs), lambda g, c: (0, 0)),
            pl.BlockSpec((SUBLANES, ts), lambda g, c: (0, g * nc + c)),
        ],
        out_shape=[
            jax.ShapeDtypeStruct((bsz * seq, d), F32),
            jax.ShapeDtypeStruct((bsz * seq, ROW_W), F32),
            jax.ShapeDtypeStruct((RT_ROWS, hs), F32),
            jax.ShapeDtypeStruct((SUBLANES, bsz * seq), F32),
        ],
        scratch_shapes=[
            pltpu.VMEM((2, tq, NB, d), F32),
            pltpu.SemaphoreType.DMA((2,)),
            pltpu.VMEM((POOL_HALO + ts, POOL_WIDTH), F32),
            pltpu.VMEM((CONV_HALO + ts, LRU_WIDTH), F32),
            pltpu.VMEM((ts, LRU_WIDTH), F32),
            pltpu.VMEM((ts, LRU_WIDTH), F32),
            pltpu.VMEM((ts, D_MODEL), F32),
            pltpu.VMEM((ts, D_MODEL), BF16),
            pltpu.VMEM((N_SLABS, ts, MXU_DIM), F32),
            pltpu.VMEM((NB, LRU_WIDTH), F32),
            pltpu.VMEM((RT_ROWS, hs), F32),
        ],
        compiler_params=pltpu.CompilerParams(
            dimension_semantics=("arbitrary", "arbitrary"), vmem_limit_bytes=VMEM_LIMIT),
        name="mixer",
    )(x, *consts)


def _dispatch_kernel(pos_ref, ztile_ref, h2e_ref, xs_ref, zbuf, sem, zsem):
    tb = h2e_ref.shape[0]
    tm = zbuf.shape[0]
    base = pl.program_id(0) * tb

    @pl.when(pl.program_id(0) == 0)
    def _():
        zbuf[...] = jnp.zeros_like(zbuf)

        def zero_copy(c):
            start = pl.multiple_of(ztile_ref[c], SUBLANES)
            return pltpu.make_async_copy(zbuf, xs_ref.at[pl.ds(start, tm)], zsem)

        for c in range(N_CLASSES):
            @pl.when(ztile_ref[c] >= 0)
            def _():
                zero_copy(c).start()
        for c in range(N_CLASSES):
            @pl.when(ztile_ref[c] >= 0)
            def _():
                zero_copy(c).wait()

        def tail_copy(k):
            return pltpu.make_async_copy(zbuf, xs_ref.at[pl.ds(pl.multiple_of(k * tm, SUBLANES), tm)], zsem)

        n_used = ztile_ref[N_CLASSES]
        n_all = xs_ref.shape[0] // tm
        lax.fori_loop(n_used, n_all, lambda k, c: (tail_copy(k).start(), c)[1], 0)
        lax.fori_loop(n_used, n_all, lambda k, c: (tail_copy(k).wait(), c)[1], 0)

    for r in range(tb):
        pltpu.make_async_copy(
            h2e_ref.at[pl.ds(r, 1)], xs_ref.at[pl.ds(pos_ref[base + r], 1)], sem).start(priority=r % 2)
    pltpu.make_async_copy(h2e_ref, xs_ref.at[pl.ds(0, tb)], sem).wait()


def _dispatch(pos, ztile, h2e, n_slots, tm):
    t = h2e.shape[0]
    tb = min(DISPATCH_TILE, t)
    assert t % tb == 0 and n_slots >= tb
    return pl.pallas_call(
        _dispatch_kernel,
        grid_spec=pltpu.PrefetchScalarGridSpec(
            num_scalar_prefetch=2,
            grid=(t // tb,),
            in_specs=[pl.BlockSpec((tb, ROW_W), lambda i, pos, zt: (i, 0))],
            out_specs=pl.BlockSpec(memory_space=pl.ANY),
            scratch_shapes=[pltpu.VMEM((tm, ROW_W), F32),
                            pltpu.SemaphoreType.DMA(()), pltpu.SemaphoreType.DMA(())],
        ),
        out_shape=jax.ShapeDtypeStruct((n_slots, ROW_W), F32),
        compiler_params=pltpu.CompilerParams(
            dimension_semantics=("arbitrary",), vmem_limit_bytes=VMEM_LIMIT),
        name="dispatch",
    )(pos, ztile, h2e)


def _expert_kernel(meta_ref, xs_ref, wg_lo, wu_lo, wd_lo, wg_hi, wu_hi, wd_hi, ys_ref):
    i = pl.program_id(0)

    @pl.when(i < meta_ref[0])
    def _():
        x = xs_ref[:, 0:D_MODEL].astype(BF16)
        w_lo = xs_ref[:, D_MODEL:D_MODEL + 1]
        w_hi = xs_ref[:, D_MODEL + 1:D_MODEL + 2]

        def ffn(wg, wu, wd, w):
            g = _dot(x, wg[0])
            u = _dot(x, wu[0])
            act = (g * _sigmoid(g)) * u * w
            return _dot(act.astype(BF16), wd[0])

        ys_ref[...] = ffn(wg_lo, wu_lo, wd_lo, w_lo) + ffn(wg_hi, wu_hi, wd_hi, w_hi)

    @pl.when(i >= meta_ref[0])
    def _():
        ys_ref[...] = jnp.zeros_like(ys_ref)


def _experts(meta, xs, e_gate, e_up, e_down, n_tiles_max):
    tm = xs.shape[0] // n_tiles_max

    def row_map(i, m):
        return (jnp.minimum(i, m[0] - 1), 0)

    def lo_map(i, m):
        return (m[1 + jnp.minimum(i, m[0] - 1)], 0, 0)

    def hi_map(i, m):
        return (m[1 + n_tiles_max + jnp.minimum(i, m[0] - 1)], 0, 0)

    up_spec = lambda mp: pl.BlockSpec((1, D_MODEL, D_EXPERT), mp)
    down_spec = lambda mp: pl.BlockSpec((1, D_EXPERT, D_MODEL), mp)
    return pl.pallas_call(
        _expert_kernel,
        grid_spec=pltpu.PrefetchScalarGridSpec(
            num_scalar_prefetch=1,
            grid=(n_tiles_max,),
            in_specs=[pl.BlockSpec((tm, ROW_W), row_map),
                      up_spec(lo_map), up_spec(lo_map), down_spec(lo_map),
                      up_spec(hi_map), up_spec(hi_map), down_spec(hi_map)],
            out_specs=pl.BlockSpec((tm, D_MODEL), lambda i, m: (i, 0)),
        ),
        out_shape=jax.ShapeDtypeStruct((xs.shape[0], D_MODEL), F32),
        compiler_params=pltpu.CompilerParams(
            dimension_semantics=("arbitrary",), vmem_limit_bytes=VMEM_LIMIT),
        name="experts",
    )(meta, xs, e_gate, e_up, e_down, e_gate, e_up, e_down)


def _combine_kernel(pos_ref, x1_ref, gf_ref, ys_ref, out_hbm, ybuf, sems, obuf, osems):
    tb = x1_ref.shape[0]
    tq = obuf.shape[1]
    nc = out_hbm.shape[1] // tq
    i = pl.program_id(0)
    slot = i % 2

    def out_copies(step, src_slot):
        g_, c_ = step // nc, step % nc
        return [pltpu.make_async_copy(obuf.at[src_slot, :, bb, :],
                                      out_hbm.at[g_ * NB + bb, pl.ds(c_ * tq, tq), :],
                                      osems.at[src_slot]) for bb in range(NB)]

    def gather(step, dst_slot):
        base = step * tb
        for r in range(tb):
            pltpu.make_async_copy(ys_ref.at[pl.ds(pos_ref[base + r], 1)],
                                  ybuf.at[dst_slot, pl.ds(r, 1)], sems.at[dst_slot]).start(priority=r % 2)

    @pl.when(i == 0)
    def _():
        gather(0, 0)

    @pl.when(i + 1 < pl.num_programs(0))
    def _():
        gather(i + 1, 1 - slot)

    pltpu.make_async_copy(ys_ref.at[pl.ds(0, tb)], ybuf.at[slot], sems.at[slot]).wait()

    @pl.when(i >= 2)
    def _():
        for cp in out_copies(i - 2, slot):
            cp.wait()

    obuf[slot] = _rms_norm(x1_ref[...] + ybuf[slot], gf_ref[...]).reshape(tq, NB, D_MODEL)
    for cp in out_copies(i, slot):
        cp.start()

    @pl.when(i + 1 == pl.num_programs(0))
    def _():
        @pl.when(i >= 1)
        def _():
            for cp in out_copies(i - 1, 1 - slot):
                cp.wait()
        for cp in out_copies(i, slot):
            cp.wait()


def _combine(pos, x1, gf, ys, bsz, seq):
    t = x1.shape[0]
    tb, tq = _mixer_tiles(bsz, seq)
    assert t % tb == 0
    return pl.pallas_call(
        _combine_kernel,
        grid_spec=pltpu.PrefetchScalarGridSpec(
            num_scalar_prefetch=1,
            grid=(t // tb,),
            in_specs=[pl.BlockSpec((tb, D_MODEL), lambda i, pos: (i, 0)),
                      pl.BlockSpec((1, D_MODEL), lambda i, pos: (0, 0)),
                      pl.BlockSpec(memory_space=pl.ANY)],
            out_specs=pl.BlockSpec(memory_space=pl.ANY),
            scratch_shapes=[pltpu.VMEM((2, tb, D_MODEL), F32), pltpu.SemaphoreType.DMA((2,)),
                            pltpu.VMEM((2, tq, NB, D_MODEL), F32), pltpu.SemaphoreType.DMA((2,))],
        ),
        out_shape=jax.ShapeDtypeStruct((bsz, seq, D_MODEL), F32),
        compiler_params=pltpu.CompilerParams(
            dimension_semantics=("arbitrary",), vmem_limit_bytes=VMEM_LIMIT),
        name="combine",
    )(pos, x1, gf, ys)


def _block_diag(w, per_block):
    n, k, _ = w.shape
    nb = n // per_block
    w = w.reshape(nb, per_block, k, k)
    eye = jnp.eye(per_block, dtype=w.dtype)
    out = jnp.einsum('bpij,pq->bpiqj', w, eye)
    return out.reshape(nb, per_block * k, per_block * k)


def _pair_tables():
    lo, hi = [], []
    for g in range(N_GROUPS):
        for a in range(EXP_PER_GROUP):
            for b in range(a + 1, EXP_PER_GROUP):
                lo.append(g * EXP_PER_GROUP + a)
                hi.append(g * EXP_PER_GROUP + b)
    return jnp.array(lo, jnp.int32), jnp.array(hi, jnp.int32)


def kernel(x, norm1_g, w_in, pool_w, pool_scale, conv_w, conv_b, rg_w_r, rg_b_r, rg_w_i, rg_b_i, rg_lambda, proj_a, proj_b, w_out, norm2_g, router_group_w, router_group_b, router_expert_w, router_expert_b, exp_w_gate, exp_w_up, exp_w_down, norm_f_g):
    bsz, seq, d = x.shape
    t = bsz * seq
    assert w_in.shape[0] == 1, "single-layer block"
    l = 0
    row = lambda v: v.reshape(1, -1).astype(F32)
    heads_per_blk = MXU_DIM // (LRU_WIDTH // rg_w_r.shape[1])
    groups_per_blk = MXU_DIM // POOL_GROUP_DIM
    n_rt = N_GROUPS + N_EXP
    wrt = jnp.concatenate([router_group_w[l].T, router_expert_w[l].T,
                           jnp.zeros((RT_ROWS - n_rt, d), F32)], axis=0)
    brt = jnp.concatenate([router_group_b[l], router_expert_b[l], jnp.zeros((RT_ROWS - n_rt,), F32)])
    hs = _mixer_tiles(bsz, seq)[0] // POST_SPLIT
    brt = jnp.broadcast_to(brt[:, None].astype(F32), (RT_ROWS, hs))

    w_in_b = w_in[l].astype(BF16)
    w_slab = jnp.concatenate([w_in_b[:, O_GATE:O_GATE + LRU_WIDTH], w_in_b[:, O_GB:O_GB + D_MODEL]], axis=1)
    w_slab = w_slab.reshape(d, N_SLABS, MXU_DIM).transpose(1, 0, 2)

    x1, h2e, cnt, route = _mixer(
        x, row(norm1_g[l]), jnp.concatenate([w_in_b[:, 0:O_GATE], w_in_b[:, O_GA:O_GB]], axis=1),
        _block_diag(pool_w[l], groups_per_blk).astype(BF16),
        row(pool_scale[l]), conv_w[l].astype(F32), row(conv_b[l]),
        _block_diag(rg_w_r[l], heads_per_blk).astype(BF16), _block_diag(rg_w_i[l], heads_per_blk).astype(BF16),
        row(rg_b_r[l]), row(rg_b_i[l]), row(rg_lambda[l]),
        proj_a[l].astype(BF16), proj_b[l].astype(BF16), w_out[l].astype(BF16), row(norm2_g[l]),
        wrt.astype(BF16), brt, w_slab)

    tm = min(EXPERT_TILE, t)
    n_tiles_max = -(-(t + N_CLASSES * (tm - 1)) // tm)
    cls = route[2].astype(jnp.int32)
    rank = route[3].astype(jnp.int32)
    counts = cnt[0:N_CLASSES, 0].astype(jnp.int32)
    padded = ((counts + tm - 1) // tm) * tm
    ends = jnp.cumsum(padded)
    offs = ends - padded
    pos = rank + jnp.sum(jnp.where(cls[:, None] == jnp.arange(N_CLASSES)[None, :], offs[None, :], 0), axis=1)
    n_tiles = ends[-1] // tm
    tile_cls = jnp.minimum(
        jnp.sum((jnp.arange(n_tiles_max)[:, None] * tm >= ends[None, :]).astype(jnp.int32), axis=1),
        N_CLASSES - 1)
    pair_lo, pair_hi = _pair_tables()
    meta = jnp.concatenate([n_tiles[None], pair_lo[tile_cls], pair_hi[tile_cls]]).astype(jnp.int32)

    ztile = jnp.concatenate([jnp.where(padded > 0, ends - tm, -1), n_tiles[None]]).astype(jnp.int32)
    xs = _dispatch(pos, ztile, h2e, n_tiles_max * tm, tm)
    ys = _experts(meta, xs, exp_w_gate[l].astype(BF16), exp_w_up[l].astype(BF16),
                  exp_w_down[l].astype(BF16), n_tiles_max)
    return _combine(pos, x1, row(norm_f_g), ys, bsz, seq)
```

```python
import jax
import jax.numpy as jnp
from jax import lax
from jax.experimental import pallas as pl
from jax.experimental.pallas import tpu as pltpu

F32 = jnp.float32
BF16 = jnp.bfloat16

LANES = 128
SUBLANES = 8
MXU_DIM = 256
NB = SUBLANES

D_MODEL = 1024
POOL_WIDTH = 512
POOL_GROUP_DIM = 128
POOL_WINDOWS = (2, 4, 8, 16)
POOL_HALO = max(POOL_WINDOWS) * NB
LRU_WIDTH = 1024
CONV_WIDTH = 4
CONV_HALO = (CONV_WIDTH - 1) * NB
RG_C = 8.0
N_GROUPS = 4
EXP_PER_GROUP = 4
N_EXP = 16
D_EXPERT = 512
EPS = 1e-6
N_PAIRS = 6
N_CLASSES = N_GROUPS * N_PAIRS

META_W = LANES
ROW_W = D_MODEL + META_W
RT_ROWS = 32

O_POOL = 0
O_LRU = POOL_WIDTH
O_GATE = O_LRU + LRU_WIDTH
O_GA = O_GATE + LRU_WIDTH
O_GB = O_GA + D_MODEL
IN_COLS = O_GB + D_MODEL
O_GA_MAIN = O_GATE

SEQ_TILE = 512
POST_SPLIT = 1
N_SLABS = 2 * LRU_WIDTH // MXU_DIM
DISPATCH_TILE = 2048
EXPERT_TILE = 512
VMEM_LIMIT = 56 * 1024 * 1024


def _sigmoid(v):
    return 0.5 * jnp.tanh(0.5 * v) + 0.5


def _gelu_tanh(v):
    c = 0.7978845608028654
    return 0.5 * v * (1.0 + jnp.tanh(c * (v + 0.044715 * (v * v * v))))


def _rms_norm(v, g):
    ms = jnp.mean(v * v, axis=-1, keepdims=True)
    return v * lax.rsqrt(ms + EPS) * g


def _dot(a, b):
    return jnp.dot(a, b, preferred_element_type=F32)


def _dot_nt(a, b):
    return lax.dot_general(a, b, (((1,), (1,)), ((), ())), preferred_element_type=F32)


def _first_index_of_max(v, idx, sentinel):
    m = jnp.max(v, axis=0, keepdims=True)
    return m, jnp.min(jnp.where(v == m, idx, sentinel), axis=0, keepdims=True)


def _route(lt, counts):
    n = lt.shape[1]
    sub = lax.broadcasted_iota(jnp.int32, (EXP_PER_GROUP, n), 0).astype(F32)
    none = jnp.float32(EXP_PER_GROUP)
    neg = jnp.float32(-3.0e38)
    gl = lt[0:N_GROUPS, :]
    gmax, gidx = _first_index_of_max(gl, sub, none)
    p_top = 1.0 / jnp.sum(jnp.exp(gl - gmax), axis=0, keepdims=True)
    el = lt[N_GROUPS + (N_GROUPS - 1) * EXP_PER_GROUP:N_GROUPS + N_GROUPS * EXP_PER_GROUP, :]
    for g in range(N_GROUPS - 2, -1, -1):
        lo = N_GROUPS + g * EXP_PER_GROUP
        el = jnp.where(gidx == float(g), lt[lo:lo + EXP_PER_GROUP, :], el)
    v1, i1 = _first_index_of_max(el, sub, none)
    v2, i2 = _first_index_of_max(jnp.where(sub == i1, neg, el), sub, none)
    ex = jnp.exp(v2 - v1)
    w1 = p_top / (1.0 + ex)
    w2 = p_top * ex / (1.0 + ex)
    first_lo = i1 < i2
    w_lo = jnp.where(first_lo, w1, w2)
    w_hi = jnp.where(first_lo, w2, w1)
    pa_ = jnp.minimum(i1, i2)
    pb_ = jnp.maximum(i1, i2)
    cls = gidx * N_PAIRS + pa_ * (7.0 - pa_) * 0.5 + (pb_ - pa_ - 1.0)

    crow = lax.broadcasted_iota(jnp.int32, (RT_ROWS, n), 0).astype(F32)
    onehot = crow == cls
    t_row = lax.broadcasted_iota(jnp.int32, (n, n), 0)
    t_col = lax.broadcasted_iota(jnp.int32, (n, n), 1)
    earlier = jnp.where(t_row < t_col, 1.0, 0.0).astype(BF16)
    prior = _dot(jnp.where(onehot, 1.0, 0.0).astype(BF16), earlier) + counts
    rank = jnp.sum(jnp.where(onehot, prior, 0.0), axis=0, keepdims=True)
    counts = counts + jnp.sum(jnp.where(onehot, 1.0, 0.0), axis=1, keepdims=True)

    mrow = lax.broadcasted_iota(jnp.int32, (SUBLANES, n), 0)
    meta = jnp.where(mrow == 0, w_lo,
                     jnp.where(mrow == 1, w_hi,
                               jnp.where(mrow == 2, cls,
                                         jnp.where(mrow == 3, rank, 0.0))))
    return meta, counts


def _mixer_kernel(x_hbm, g1_ref, w_in_ref, pool_bd_ref, pool_scale_ref, conv_w_ref, conv_b_ref,
                  wr_ref, wi_ref, br_ref, bi_ref, lam_ref, pa_ref, pb_ref, wo_ref, g2_ref,
                  wrt_ref, brt_ref, w_slab_ref,
                  x1_ref, h2e_ref, cnt_ref, route_ref,
                  xbuf, xsem, pool_ext, lru_ext, a_buf, b_buf, mixa_buf, h_buf, zg_buf, h_carry, cnt_scr):
    g = pl.program_id(0)
    c = pl.program_id(1)
    nc = pl.num_programs(1)
    tq = xbuf.shape[1]
    ts = tq * NB
    step = g * nc + c
    slot = step % 2

    def x_copies(g_, c_, slot_):
        return [pltpu.make_async_copy(x_hbm.at[g_ * NB + bb, pl.ds(c_ * tq, tq), :],
                                      xbuf.at[slot_, :, bb, :], xsem.at[slot_]) for bb in range(NB)]

    @pl.when(step == 0)
    def _():
        for cp in x_copies(g, c, slot):
            cp.start()

    @pl.when(step + 1 < pl.num_programs(0) * nc)
    def _():
        wrap = c + 1 == nc
        for cp in x_copies(jnp.where(wrap, g + 1, g), jnp.where(wrap, 0, c + 1), 1 - slot):
            cp.start()

    @pl.when(c == 0)
    def _():
        pool_ext[0:POOL_HALO, :] = jnp.zeros((POOL_HALO, POOL_WIDTH), F32)
        lru_ext[0:CONV_HALO, :] = jnp.zeros((CONV_HALO, LRU_WIDTH), F32)
        h_carry[...] = jnp.zeros_like(h_carry)

    @pl.when(step == 0)
    def _():
        cnt_scr[...] = jnp.zeros_like(cnt_scr)

    for cp in x_copies(g, c, slot):
        cp.wait()
    x = xbuf[slot].reshape(ts, D_MODEL)
    h = _rms_norm(x, g1_ref[...]).astype(BF16)
    h_buf[...] = h

    pool_ext[POOL_HALO:POOL_HALO + ts, :] = _dot(h, w_in_ref[:, O_POOL:O_POOL + POOL_WIDTH])
    ext = pool_ext[...]
    pool_ext[0:POOL_HALO, :] = ext[ts:ts + POOL_HALO, :]
    t_idx = lax.broadcasted_iota(jnp.int32, (ts, POOL_GROUP_DIM), 0) // NB
    pos1 = (c * tq + 1 + t_idx).astype(F32)
    zs = []
    for gi, w in enumerate(POOL_WINDOWS):
        s = ext[:, gi * POOL_GROUP_DIM:(gi + 1) * POOL_GROUP_DIM]
        u = s[POOL_HALO:]
        d = 1
        while d < w:
            s = s[d * NB:] + s[:-d * NB]
            d *= 2
        off = POOL_HALO - (w - 1) * NB
        win = s[off:off + ts]
        zs.append(win / jnp.minimum(pos1, float(w)) - u)
    z = jnp.concatenate(zs, axis=1).astype(BF16)
    y_pool = jnp.concatenate(
        [_dot(z[:, j * MXU_DIM:(j + 1) * MXU_DIM], pool_bd_ref[j]) for j in range(POOL_WIDTH // MXU_DIM)],
        axis=1) * pool_scale_ref[...]
    y_a = _dot(y_pool.astype(BF16), pa_ref[...])
    mixa_buf[...] = _sigmoid(_dot(h, w_in_ref[:, O_GA_MAIN:O_GA_MAIN + D_MODEL])) * y_a

    lru_ext[CONV_HALO:CONV_HALO + ts, :] = _dot(h, w_in_ref[:, O_LRU:O_LRU + LRU_WIDTH])
    xc = conv_b_ref[...] + conv_w_ref[CONV_WIDTH - 1:CONV_WIDTH, :] * lru_ext[CONV_HALO:CONV_HALO + ts, :]
    for k in range(CONV_WIDTH - 1):
        o = k * NB
        xc = xc + conv_w_ref[k:k + 1, :] * lru_ext[o:o + ts, :]
    lru_ext[0:CONV_HALO, :] = lru_ext[ts:ts + CONV_HALO, :]
    xcb = xc.astype(BF16)
    nblk = LRU_WIDTH // MXU_DIM
    r = _sigmoid(jnp.concatenate(
        [_dot(xcb[:, j * MXU_DIM:(j + 1) * MXU_DIM], wr_ref[j]) for j in range(nblk)], axis=1) + br_ref[...])
    i = _sigmoid(jnp.concatenate(
        [_dot(xcb[:, j * MXU_DIM:(j + 1) * MXU_DIM], wi_ref[j]) for j in range(nblk)], axis=1) + bi_ref[...])
    nlam = -lam_ref[...]
    softplus = jnp.maximum(nlam, 0.0) + jnp.log1p(jnp.exp(-jnp.abs(nlam)))
    log_a = (-RG_C * softplus) * r
    a = jnp.exp(log_a)
    a_buf[...] = a
    v = jnp.tanh(-log_a) * (1.0 + a * a)
    b_buf[...] = jnp.where(v > 0.0, v * lax.rsqrt(v), 0.0) * (i * xc)

    steps_per_trip = tq // N_SLABS
    carry = h_carry[...]
    for j in range(N_SLABS):
        zg_buf[j] = _dot(h_buf[...], w_slab_ref[j])
        for q in range(steps_per_trip):
            r0 = (j * steps_per_trip + q) * NB
            carry = b_buf[r0:r0 + NB, :] + a_buf[r0:r0 + NB, :] * carry
            b_buf[r0:r0 + NB, :] = carry
    h_carry[...] = carry

    hs = ts // POST_SPLIT
    counts = cnt_scr[...]
    for part in range(POST_SPLIT):
        rows = slice(part * hs, (part + 1) * hs)
        half = N_SLABS // 2
        z_gate = jnp.concatenate([zg_buf[j, rows, :] for j in range(half)], axis=1)
        z_gb = jnp.concatenate([zg_buf[half + j, rows, :] for j in range(half)], axis=1)
        y_lru = b_buf[rows, :] * _gelu_tanh(z_gate)
        y_b = _dot(y_lru.astype(BF16), pb_ref[...])
        mix = mixa_buf[rows, :] + _sigmoid(z_gb) * y_b
        x1 = x[rows, :] + _dot(mix.astype(BF16), wo_ref[...])
        x1_ref[rows, :] = x1
        h2 = _rms_norm(x1, g2_ref[...])
        h2e_ref[rows, 0:D_MODEL] = h2
        lt = _dot_nt(wrt_ref[...], h2.astype(BF16)) + brt_ref[...]
        meta, counts = _route(lt, counts)
        route_ref[:, rows] = meta
        meta_t = jnp.concatenate([meta, jnp.zeros((META_W - SUBLANES, hs), F32)], axis=0).T
        h2e_ref[rows, D_MODEL:ROW_W] = meta_t
    cnt_scr[...] = counts
    cnt_ref[...] = counts


def _const_spec(shape):
    zeros = (0,) * len(shape)
    return pl.BlockSpec(shape, lambda b, c: zeros, pipeline_mode=pl.Buffered(1))


def _mixer_tiles(bsz, seq):
    tq = min(SEQ_TILE // NB, seq)
    ts = tq * NB
    assert bsz % NB == 0 and seq % tq == 0 and tq % N_SLABS == 0 and ts >= POOL_HALO
    return ts, tq


def _mixer(x, g1, w_in, pool_bd, pool_scale, conv_w, conv_b, wr4, wi4, b_r, b_i, lam, pa, pb, wo, g2,
           wrt, brt, w_slab):
    bsz, seq, d = x.shape
    ts, tq = _mixer_tiles(bsz, seq)
    hs = ts // POST_SPLIT
    assert d == D_MODEL and hs % LANES == 0 and brt.shape == (RT_ROWS, hs)
    nc = seq // tq
    consts = (g1, w_in, pool_bd, pool_scale, conv_w, conv_b, wr4, wi4, b_r, b_i, lam, pa, pb, wo, g2,
              wrt, brt, w_slab)
    return pl.pallas_call(
        _mixer_kernel,
        grid=(bsz // NB, nc),
        in_specs=[pl.BlockSpec(memory_space=pl.ANY)] + [_const_spec(a.shape) for a in consts],
        out_specs=[
            pl.BlockSpec((ts, d), lambda g, c: (g * nc + c, 0)),
            pl.BlockSpec((ts, ROW_W), lambda g, c: (g * nc + c, 0)),
            pl.BlockSpec((RT_ROWS, hs), lambda g, c: (0, 0)),
            pl.BlockSpec((SUBLANES, ts), lambda g, c: (0, g * nc + c)),
        ],
        out_shape=[
            jax.ShapeDtypeStruct((bsz * seq, d), F32),
            jax.ShapeDtypeStruct((bsz * seq, ROW_W), F32),
            jax.ShapeDtypeStruct((RT_ROWS, hs), F32),
            jax.ShapeDtypeStruct((SUBLANES, bsz * seq), F32),
        ],
        scratch_shapes=[
            pltpu.VMEM((2, tq, NB, d), F32),
            pltpu.SemaphoreType.DMA((2,)),
            pltpu.VMEM((POOL_HALO + ts, POOL_WIDTH), F32),
            pltpu.VMEM((CONV_HALO + ts, LRU_WIDTH), F32),
            pltpu.VMEM((ts, LRU_WIDTH), F32),
            pltpu.VMEM((ts, LRU_WIDTH), F32),
            pltpu.VMEM((ts, D_MODEL), F32),
            pltpu.VMEM((ts, D_MODEL), BF16),
            pltpu.VMEM((N_SLABS, ts, MXU_DIM), F32),
            pltpu.VMEM((NB, LRU_WIDTH), F32),
            pltpu.VMEM((RT_ROWS, hs), F32),
        ],
        compiler_params=pltpu.CompilerParams(
            dimension_semantics=("arbitrary", "arbitrary"), vmem_limit_bytes=VMEM_LIMIT),
        name="mixer",
    )(x, *consts)


def _dispatch_kernel(pos_ref, ztile_ref, h2e_ref, xs_ref, zbuf, sem, zsem):
    tb = h2e_ref.shape[0]
    tm = zbuf.shape[0]
    base = pl.program_id(0) * tb

    @pl.when(pl.program_id(0) == 0)
    def _():
        zbuf[...] = jnp.zeros_like(zbuf)

        def zero_copy(c):
            start = pl.multiple_of(ztile_ref[c], SUBLANES)
            return pltpu.make_async_copy(zbuf, xs_ref.at[pl.ds(start, tm)], zsem)

        for c in range(N_CLASSES):
            @pl.when(ztile_ref[c] >= 0)
            def _():
                zero_copy(c).start()
        for c in range(N_CLASSES):
            @pl.when(ztile_ref[c] >= 0)
            def _():
                zero_copy(c).wait()

        def tail_copy(k):
            return pltpu.make_async_copy(zbuf, xs_ref.at[pl.ds(pl.multiple_of(k * tm, SUBLANES), tm)], zsem)

        n_used = ztile_ref[N_CLASSES]
        n_all = xs_ref.shape[0] // tm
        lax.fori_loop(n_used, n_all, lambda k, c: (tail_copy(k).start(), c)[1], 0)
        lax.fori_loop(n_used, n_all, lambda k, c: (tail_copy(k).wait(), c)[1], 0)

    for r in range(tb):
        pltpu.make_async_copy(
            h2e_ref.at[pl.ds(r, 1)], xs_ref.at[pl.ds(pos_ref[base + r], 1)], sem).start(priority=r % 2)
    pltpu.make_async_copy(h2e_ref, xs_ref.at[pl.ds(0, tb)], sem).wait()


def _dispatch(pos, ztile, h2e, n_slots, tm):
    t = h2e.shape[0]
    tb = min(DISPATCH_TILE, t)
    assert t % tb == 0 and n_slots >= tb
    return pl.pallas_call(
        _dispatch_kernel,
        grid_spec=pltpu.PrefetchScalarGridSpec(
            num_scalar_prefetch=2,
            grid=(t // tb,),
            in_specs=[pl.BlockSpec((tb, ROW_W), lambda i, pos, zt: (i, 0))],
            out_specs=pl.BlockSpec(memory_space=pl.ANY),
            scratch_shapes=[pltpu.VMEM((tm, ROW_W), F32),
                            pltpu.SemaphoreType.DMA(()), pltpu.SemaphoreType.DMA(())],
        ),
        out_shape=jax.ShapeDtypeStruct((n_slots, ROW_W), F32),
        compiler_params=pltpu.CompilerParams(
            dimension_semantics=("arbitrary",), vmem_limit_bytes=VMEM_LIMIT),
        name="dispatch",
    )(pos, ztile, h2e)


def _expert_kernel(meta_ref, xs_ref, wg_lo, wu_lo, wd_lo, wg_hi, wu_hi, wd_hi, ys_ref):
    i = pl.program_id(0)

    @pl.when(i < meta_ref[0])
    def _():
        x = xs_ref[:, 0:D_MODEL].astype(BF16)
        w_lo = xs_ref[:, D_MODEL:D_MODEL + 1]
        w_hi = xs_ref[:, D_MODEL + 1:D_MODEL + 2]

        def ffn(wg, wu, wd, w):
            g = _dot(x, wg[0].astype(BF16))
            u = _dot(x, wu[0].astype(BF16))
            act = (g * _sigmoid(g)) * u * w
            return _dot(act.astype(BF16), wd[0].astype(BF16))

        ys_ref[...] = ffn(wg_lo, wu_lo, wd_lo, w_lo) + ffn(wg_hi, wu_hi, wd_hi, w_hi)

    @pl.when(i >= meta_ref[0])
    def _():
        ys_ref[...] = jnp.zeros_like(ys_ref)


def _experts(meta, xs, e_gate, e_up, e_down, n_tiles_max):
    tm = xs.shape[0] // n_tiles_max

    def row_map(i, m):
        return (jnp.minimum(i, m[0] - 1), 0)

    def lo_map(i, m):
        return (m[1 + jnp.minimum(i, m[0] - 1)], 0, 0)

    def hi_map(i, m):
        return (m[1 + n_tiles_max + jnp.minimum(i, m[0] - 1)], 0, 0)

    up_spec = lambda mp: pl.BlockSpec((1, D_MODEL, D_EXPERT), mp)
    down_spec = lambda mp: pl.BlockSpec((1, D_EXPERT, D_MODEL), mp)
    return pl.pallas_call(
        _expert_kernel,
        grid_spec=pltpu.PrefetchScalarGridSpec(
            num_scalar_prefetch=1,
            grid=(n_tiles_max,),
            in_specs=[pl.BlockSpec((tm, ROW_W), row_map),
                      up_spec(lo_map), up_spec(lo_map), down_spec(lo_map),
                      up_spec(hi_map), up_spec(hi_map), down_spec(hi_map)],
            out_specs=pl.BlockSpec((tm, D_MODEL), lambda i, m: (i, 0)),
        ),
        out_shape=jax.ShapeDtypeStruct((xs.shape[0], D_MODEL), F32),
        compiler_params=pltpu.CompilerParams(
            dimension_semantics=("arbitrary",), vmem_limit_bytes=VMEM_LIMIT),
        name="experts",
    )(meta, xs, e_gate, e_up, e_down, e_gate, e_up, e_down)


def _combine_kernel(pos_ref, x1_ref, gf_ref, ys_ref, out_hbm, ybuf, sems, obuf, osems):
    tb = ybuf.shape[1]
    tq = obuf.shape[1]
    nc = out_hbm.shape[1] // tq
    i = pl.program_id(0)
    last = i + 1 == pl.num_programs(0)

    def out_copies(blk, slot):
        g_, c_ = blk // nc, blk % nc
        return [pltpu.make_async_copy(obuf.at[slot, :, bb, :],
                                      out_hbm.at[g_ * NB + bb, pl.ds(c_ * tq, tq), :],
                                      osems.at[slot]) for bb in range(NB)]

    def gather(blk, slot):
        base = blk * tb
        for r in range(tb):
            pltpu.make_async_copy(ys_ref.at[pl.ds(pos_ref[base + r], 1)],
                                  ybuf.at[slot, pl.ds(r, 1)], sems.at[slot]).start(priority=r % 2)

    def finish(blk, slot):
        pltpu.make_async_copy(ys_ref.at[pl.ds(0, tb)], ybuf.at[slot], sems.at[slot]).wait()

        @pl.when(i >= 1)
        def _():
            for cp in out_copies(blk - 2, slot):
                cp.wait()

        rows = slice(slot * tb, (slot + 1) * tb)
        obuf[slot] = _rms_norm(x1_ref[rows, :] + ybuf[slot], gf_ref[...]).reshape(tq, NB, D_MODEL)
        for cp in out_copies(blk, slot):
            cp.start()

        @pl.when(jnp.logical_not(last))
        def _():
            gather(blk + 2, slot)

    @pl.when(i == 0)
    def _():
        gather(0, 0)
        gather(1, 1)

    finish(2 * i, 0)
    finish(2 * i + 1, 1)

    @pl.when(last)
    def _():
        for slot in range(2):
            for cp in out_copies(2 * i + slot, slot):
                cp.wait()


def _combine(pos, x1, gf, ys, bsz, seq):
    t = x1.shape[0]
    tb, tq = _mixer_tiles(bsz, seq)
    assert t % (2 * tb) == 0
    return pl.pallas_call(
        _combine_kernel,
        grid_spec=pltpu.PrefetchScalarGridSpec(
            num_scalar_prefetch=1,
            grid=(t // (2 * tb),),
            in_specs=[pl.BlockSpec((2 * tb, D_MODEL), lambda i, pos: (i, 0)),
                      pl.BlockSpec((1, D_MODEL), lambda i, pos: (0, 0)),
                      pl.BlockSpec(memory_space=pl.ANY)],
            out_specs=pl.BlockSpec(memory_space=pl.ANY),
            scratch_shapes=[pltpu.VMEM((2, tb, D_MODEL), F32), pltpu.SemaphoreType.DMA((2,)),
                            pltpu.VMEM((2, tq, NB, D_MODEL), F32), pltpu.SemaphoreType.DMA((2,))],
        ),
        out_shape=jax.ShapeDtypeStruct((bsz, seq, D_MODEL), F32),
        compiler_params=pltpu.CompilerParams(
            dimension_semantics=("arbitrary",), vmem_limit_bytes=VMEM_LIMIT),
        name="combine",
    )(pos, x1, gf, ys)


def _block_diag(w, per_block):
    n, k, _ = w.shape
    nb = n // per_block
    w = w.reshape(nb, per_block, k, k)
    eye = jnp.eye(per_block, dtype=w.dtype)
    out = jnp.einsum('bpij,pq->bpiqj', w, eye)
    return out.reshape(nb, per_block * k, per_block * k)


def _pair_tables():
    lo, hi = [], []
    for g in range(N_GROUPS):
        for a in range(EXP_PER_GROUP):
            for b in range(a + 1, EXP_PER_GROUP):
                lo.append(g * EXP_PER_GROUP + a)
                hi.append(g * EXP_PER_GROUP + b)
    return jnp.array(lo, jnp.int32), jnp.array(hi, jnp.int32)


def kernel(x, norm1_g, w_in, pool_w, pool_scale, conv_w, conv_b, rg_w_r, rg_b_r, rg_w_i, rg_b_i, rg_lambda, proj_a, proj_b, w_out, norm2_g, router_group_w, router_group_b, router_expert_w, router_expert_b, exp_w_gate, exp_w_up, exp_w_down, norm_f_g):
    bsz, seq, d = x.shape
    t = bsz * seq
    assert w_in.shape[0] == 1, "single-layer block"
    l = 0
    row = lambda v: v.reshape(1, -1).astype(F32)
    heads_per_blk = MXU_DIM // (LRU_WIDTH // rg_w_r.shape[1])
    groups_per_blk = MXU_DIM // POOL_GROUP_DIM
    n_rt = N_GROUPS + N_EXP
    wrt = jnp.concatenate([router_group_w[l].T, router_expert_w[l].T,
                           jnp.zeros((RT_ROWS - n_rt, d), F32)], axis=0)
    brt = jnp.concatenate([router_group_b[l], router_expert_b[l], jnp.zeros((RT_ROWS - n_rt,), F32)])
    hs = _mixer_tiles(bsz, seq)[0] // POST_SPLIT
    brt = jnp.broadcast_to(brt[:, None].astype(F32), (RT_ROWS, hs))

    w_in_b = w_in[l].astype(BF16)
    w_slab = jnp.concatenate([w_in_b[:, O_GATE:O_GATE + LRU_WIDTH], w_in_b[:, O_GB:O_GB + D_MODEL]], axis=1)
    w_slab = w_slab.reshape(d, N_SLABS, MXU_DIM).transpose(1, 0, 2)

    x1, h2e, cnt, route = _mixer(
        x, row(norm1_g[l]), jnp.concatenate([w_in_b[:, 0:O_GATE], w_in_b[:, O_GA:O_GB]], axis=1),
        _block_diag(pool_w[l], groups_per_blk).astype(BF16),
        row(pool_scale[l]), conv_w[l].astype(F32), row(conv_b[l]),
        _block_diag(rg_w_r[l], heads_per_blk).astype(BF16), _block_diag(rg_w_i[l], heads_per_blk).astype(BF16),
        row(rg_b_r[l]), row(rg_b_i[l]), row(rg_lambda[l]),
        proj_a[l].astype(BF16), proj_b[l].astype(BF16), w_out[l].astype(BF16), row(norm2_g[l]),
        wrt.astype(BF16), brt, w_slab)

    tm = min(EXPERT_TILE, t)
    n_tiles_max = -(-(t + N_CLASSES * (tm - 1)) // tm)
    cls = route[2].astype(jnp.int32)
    rank = route[3].astype(jnp.int32)
    counts = cnt[0:N_CLASSES, 0].astype(jnp.int32)
    padded = ((counts + tm - 1) // tm) * tm
    ends = jnp.cumsum(padded)
    offs = ends - padded
    pos = rank + jnp.sum(jnp.where(cls[:, None] == jnp.arange(N_CLASSES)[None, :], offs[None, :], 0), axis=1)
    n_tiles = ends[-1] // tm
    tile_cls = jnp.minimum(
        jnp.sum((jnp.arange(n_tiles_max)[:, None] * tm >= ends[None, :]).astype(jnp.int32), axis=1),
        N_CLASSES - 1)
    pair_lo, pair_hi = _pair_tables()
    meta = jnp.concatenate([n_tiles[None], pair_lo[tile_cls], pair_hi[tile_cls]]).astype(jnp.int32)

    ztile = jnp.concatenate([jnp.where(padded > 0, ends - tm, -1), n_tiles[None]]).astype(jnp.int32)
    xs = _dispatch(pos, ztile, h2e, n_tiles_max * tm, tm)
    ys = _experts(meta, xs, exp_w_gate[l], exp_w_up[l], exp_w_down[l], n_tiles_max)
    return _combine(pos, x1, row(norm_f_g), ys, bsz, seq)
```

```python
import jax
import jax.numpy as jnp
from jax import lax
from jax.experimental import pallas as pl
from jax.experimental.pallas import tpu as pltpu

F32 = jnp.float32
BF16 = jnp.bfloat16

LANES = 128
SUBLANES = 8
MXU_DIM = 256
NB = SUBLANES

D_MODEL = 1024
POOL_WIDTH = 512
POOL_GROUP_DIM = 128
POOL_WINDOWS = (2, 4, 8, 16)
POOL_HALO = max(POOL_WINDOWS) * NB
LRU_WIDTH = 1024
CONV_WIDTH = 4
CONV_HALO = (CONV_WIDTH - 1) * NB
RG_C = 8.0
LOG2_E = 1.4426950408889634
N_GROUPS = 4
EXP_PER_GROUP = 4
N_EXP = 16
D_EXPERT = 512
EPS = 1e-6
N_PAIRS = 6
N_CLASSES = N_GROUPS * N_PAIRS

META_W = LANES
ROW_W = D_MODEL + META_W
RT_ROWS = 32

O_POOL = 0
O_LRU = POOL_WIDTH
O_GATE = O_LRU + LRU_WIDTH
O_GA = O_GATE + LRU_WIDTH
O_GB = O_GA + D_MODEL
IN_COLS = O_GB + D_MODEL
O_GA_MAIN = O_GATE

SEQ_TILE = 512
POST_SPLIT = 1
N_SLABS = 2 * LRU_WIDTH // MXU_DIM
DISPATCH_TILE = 2048
EXPERT_TILE = 512
VMEM_LIMIT = 56 * 1024 * 1024


def _sigmoid(v):
    return 0.5 * jnp.tanh(0.5 * v) + 0.5


def _sigmoid_of_half(hv):
    return 0.5 * jnp.tanh(hv) + 0.5


def _gelu_tanh_of_half(hv):
    c = 0.7978845608028654
    k0 = 2.0 * c
    k1 = 8.0 * 0.044715 * c
    return hv * (1.0 + jnp.tanh(hv * (k1 * (hv * hv) + k0)))


def _rms_norm(v, g):
    ms = jnp.mean(v * v, axis=-1, keepdims=True)
    return v * lax.rsqrt(ms + EPS) * g


def _dot(a, b):
    return jnp.dot(a, b, preferred_element_type=F32)


def _dot_nt(a, b):
    return lax.dot_general(a, b, (((1,), (1,)), ((), ())), preferred_element_type=F32)


def _first_index_of_max(v, idx, sentinel):
    m = jnp.max(v, axis=0, keepdims=True)
    return m, jnp.min(jnp.where(v == m, idx, sentinel), axis=0, keepdims=True)


def _route(lt, counts):
    n = lt.shape[1]
    sub = lax.broadcasted_iota(jnp.int32, (EXP_PER_GROUP, n), 0).astype(F32)
    none = jnp.float32(EXP_PER_GROUP)
    neg = jnp.float32(-3.0e38)
    gl = lt[0:N_GROUPS, :]
    gmax, gidx = _first_index_of_max(gl, sub, none)
    p_top = 1.0 / jnp.sum(jnp.exp(gl - gmax), axis=0, keepdims=True)
    el = lt[N_GROUPS + (N_GROUPS - 1) * EXP_PER_GROUP:N_GROUPS + N_GROUPS * EXP_PER_GROUP, :]
    for g in range(N_GROUPS - 2, -1, -1):
        lo = N_GROUPS + g * EXP_PER_GROUP
        el = jnp.where(gidx == float(g), lt[lo:lo + EXP_PER_GROUP, :], el)
    v1, i1 = _first_index_of_max(el, sub, none)
    v2, i2 = _first_index_of_max(jnp.where(sub == i1, neg, el), sub, none)
    ex = jnp.exp(v2 - v1)
    w1 = p_top / (1.0 + ex)
    w2 = p_top * ex / (1.0 + ex)
    first_lo = i1 < i2
    w_lo = jnp.where(first_lo, w1, w2)
    w_hi = jnp.where(first_lo, w2, w1)
    pa_ = jnp.minimum(i1, i2)
    pb_ = jnp.maximum(i1, i2)
    cls = gidx * N_PAIRS + pa_ * (7.0 - pa_) * 0.5 + (pb_ - pa_ - 1.0)

    crow = lax.broadcasted_iota(jnp.int32, (RT_ROWS, n), 0).astype(F32)
    onehot = crow == cls
    t_row = lax.broadcasted_iota(jnp.int32, (n, n), 0)
    t_col = lax.broadcasted_iota(jnp.int32, (n, n), 1)
    earlier = jnp.where(t_row < t_col, 1.0, 0.0).astype(BF16)
    prior = _dot(jnp.where(onehot, 1.0, 0.0).astype(BF16), earlier) + counts
    rank = jnp.sum(jnp.where(onehot, prior, 0.0), axis=0, keepdims=True)
    counts = counts + jnp.sum(jnp.where(onehot, 1.0, 0.0), axis=1, keepdims=True)

    mrow = lax.broadcasted_iota(jnp.int32, (SUBLANES, n), 0)
    meta = jnp.where(mrow == 0, w_lo,
                     jnp.where(mrow == 1, w_hi,
                               jnp.where(mrow == 2, cls,
                                         jnp.where(mrow == 3, rank, 0.0))))
    return meta, counts


def _mixer_kernel(x_hbm, g1_ref, w_in_ref, pool_bd_ref, pool_scale_ref, conv_w_ref, conv_b_ref,
                  wr_ref, wi_ref, br_ref, bi_ref, lam_ref, pa_ref, pb_ref, wo_ref, g2_ref,
                  wrt_ref, brt_ref, w_slab_ref,
                  x1_ref, h2e_ref, cnt_ref, route_ref,
                  xbuf, xsem, pool_ext, lru_ext, a_buf, b_buf, mixa_buf, h_buf, zg_buf, h_carry, cnt_scr):
    g = pl.program_id(0)
    c = pl.program_id(1)
    nc = pl.num_programs(1)
    tq = xbuf.shape[1]
    ts = tq * NB
    step = g * nc + c
    slot = step % 2

    def x_copies(g_, c_, slot_):
        return [pltpu.make_async_copy(x_hbm.at[g_ * NB + bb, pl.ds(c_ * tq, tq), :],
                                      xbuf.at[slot_, :, bb, :], xsem.at[slot_]) for bb in range(NB)]

    @pl.when(step == 0)
    def _():
        for cp in x_copies(g, c, slot):
            cp.start()

    @pl.when(step + 1 < pl.num_programs(0) * nc)
    def _():
        wrap = c + 1 == nc
        for cp in x_copies(jnp.where(wrap, g + 1, g), jnp.where(wrap, 0, c + 1), 1 - slot):
            cp.start()

    @pl.when(c == 0)
    def _():
        pool_ext[0:POOL_HALO, :] = jnp.zeros((POOL_HALO, POOL_WIDTH), F32)
        lru_ext[0:CONV_HALO, :] = jnp.zeros((CONV_HALO, LRU_WIDTH), F32)
        h_carry[...] = jnp.zeros_like(h_carry)

    @pl.when(step == 0)
    def _():
        cnt_scr[...] = jnp.zeros_like(cnt_scr)

    for cp in x_copies(g, c, slot):
        cp.wait()
    x = xbuf[slot].reshape(ts, D_MODEL)
    h = _rms_norm(x, g1_ref[...]).astype(BF16)
    h_buf[...] = h

    pool_ext[POOL_HALO:POOL_HALO + ts, :] = _dot(h, w_in_ref[:, O_POOL:O_POOL + POOL_WIDTH])
    ext = pool_ext[...]
    pool_ext[0:POOL_HALO, :] = ext[ts:ts + POOL_HALO, :]
    t_idx = lax.broadcasted_iota(jnp.int32, (ts, POOL_GROUP_DIM), 0) // NB
    pos1 = (c * tq + 1 + t_idx).astype(F32)
    zs = []
    for gi, w in enumerate(POOL_WINDOWS):
        s = ext[:, gi * POOL_GROUP_DIM:(gi + 1) * POOL_GROUP_DIM]
        u = s[POOL_HALO:]
        d = 1
        while d < w:
            s = s[d * NB:] + s[:-d * NB]
            d *= 2
        off = POOL_HALO - (w - 1) * NB
        win = s[off:off + ts]
        zs.append(win / jnp.minimum(pos1, float(w)) - u)
    z = jnp.concatenate(zs, axis=1).astype(BF16)
    y_pool = jnp.concatenate(
        [_dot(z[:, j * MXU_DIM:(j + 1) * MXU_DIM], pool_bd_ref[j]) for j in range(POOL_WIDTH // MXU_DIM)],
        axis=1) * pool_scale_ref[...]
    y_a = _dot(y_pool.astype(BF16), pa_ref[...])
    mixa_buf[...] = _sigmoid_of_half(_dot(h, w_in_ref[:, O_GA_MAIN:O_GA_MAIN + D_MODEL])) * y_a

    lru_ext[CONV_HALO:CONV_HALO + ts, :] = _dot(h, w_in_ref[:, O_LRU:O_LRU + LRU_WIDTH])
    xc = conv_b_ref[...] + conv_w_ref[CONV_WIDTH - 1:CONV_WIDTH, :] * lru_ext[CONV_HALO:CONV_HALO + ts, :]
    for k in range(CONV_WIDTH - 1):
        o = k * NB
        xc = xc + conv_w_ref[k:k + 1, :] * lru_ext[o:o + ts, :]
    lru_ext[0:CONV_HALO, :] = lru_ext[ts:ts + CONV_HALO, :]
    xcb = xc.astype(BF16)
    nblk = LRU_WIDTH // MXU_DIM
    t_r = jnp.tanh(jnp.concatenate(
        [_dot(xcb[:, j * MXU_DIM:(j + 1) * MXU_DIM], wr_ref[j]) for j in range(nblk)], axis=1) + br_ref[...])
    i = _sigmoid_of_half(jnp.concatenate(
        [_dot(xcb[:, j * MXU_DIM:(j + 1) * MXU_DIM], wi_ref[j]) for j in range(nblk)], axis=1) + bi_ref[...])
    nlam = -lam_ref[...]
    softplus = jnp.maximum(nlam, 0.0) + jnp.log1p(jnp.exp(-jnp.abs(nlam)))
    half_c = (0.5 * RG_C) * softplus
    y = half_c * t_r + half_c
    a = jnp.exp2(y * (-LOG2_E))
    a_buf[...] = a
    v = jnp.tanh(y) * (1.0 + a * a)
    b_buf[...] = jnp.where(v > 0.0, v * lax.rsqrt(v), 0.0) * (i * xc)

    steps_per_trip = tq // N_SLABS
    carry = h_carry[...]
    for j in range(N_SLABS):
        zg_buf[j] = _dot(h_buf[...], w_slab_ref[j])
        for q in range(steps_per_trip):
            r0 = (j * steps_per_trip + q) * NB
            carry = b_buf[r0:r0 + NB, :] + a_buf[r0:r0 + NB, :] * carry
            b_buf[r0:r0 + NB, :] = carry
    h_carry[...] = carry

    hs = ts // POST_SPLIT
    counts = cnt_scr[...]
    for part in range(POST_SPLIT):
        rows = slice(part * hs, (part + 1) * hs)
        half = N_SLABS // 2
        z_gate = jnp.concatenate([zg_buf[j, rows, :] for j in range(half)], axis=1)
        z_gb = jnp.concatenate([zg_buf[half + j, rows, :] for j in range(half)], axis=1)
        y_lru = b_buf[rows, :] * _gelu_tanh_of_half(z_gate)
        y_b = _dot(y_lru.astype(BF16), pb_ref[...])
        mix = mixa_buf[rows, :] + _sigmoid_of_half(z_gb) * y_b
        x1 = x[rows, :] + _dot(mix.astype(BF16), wo_ref[...])
        x1_ref[rows, :] = x1
        h2 = _rms_norm(x1, g2_ref[...])
        h2e_ref[rows, 0:D_MODEL] = h2
        lt = _dot_nt(wrt_ref[...], h2.astype(BF16)) + brt_ref[...]
        meta, counts = _route(lt, counts)
        route_ref[:, rows] = meta
        meta_t = jnp.concatenate([meta, jnp.zeros((META_W - SUBLANES, hs), F32)], axis=0).T
        h2e_ref[rows, D_MODEL:ROW_W] = meta_t
    cnt_scr[...] = counts
    cnt_ref[...] = counts


def _const_spec(shape):
    zeros = (0,) * len(shape)
    return pl.BlockSpec(shape, lambda b, c: zeros, pipeline_mode=pl.Buffered(1))


def _mixer_tiles(bsz, seq):
    tq = min(SEQ_TILE // NB, seq)
    ts = tq * NB
    assert bsz % NB == 0 and seq % tq == 0 and tq % N_SLABS == 0 and ts >= POOL_HALO
    return ts, tq


def _mixer(x, g1, w_in, pool_bd, pool_scale, conv_w, conv_b, wr4, wi4, b_r, b_i, lam, pa, pb, wo, g2,
           wrt, brt, w_slab):
    bsz, seq, d = x.shape
    ts, tq = _mixer_tiles(bsz, seq)
    hs = ts // POST_SPLIT
    assert d == D_MODEL and hs % LANES == 0 and brt.shape == (RT_ROWS, hs)
    nc = seq // tq
    consts = (g1, w_in, pool_bd, pool_scale, conv_w, conv_b, wr4, wi4, b_r, b_i, lam, pa, pb, wo, g2,
              wrt, brt, w_slab)
    return pl.pallas_call(
        _mixer_kernel,
        grid=(bsz // NB, nc),
        in_specs=[pl.BlockSpec(memory_space=pl.ANY)] + [_const_spec(a.shape) for a in consts],
        out_specs=[
            pl.BlockSpec((ts, d), lambda g, c: (g * nc + c, 0)),
            pl.BlockSpec((ts, ROW_W), lambda g, c: (g * nc + c, 0)),
            pl.BlockSpec((RT_ROWS, hs), lambda g, c: (0, 0)),
            pl.BlockSpec((SUBLANES, ts), lambda g, c: (0, g * nc + c)),
        ],
        out_shape=[
            jax.ShapeDtypeStruct((bsz * seq, d), F32),
            jax.ShapeDtypeStruct((bsz * seq, ROW_W), F32),
            jax.ShapeDtypeStruct((RT_ROWS, hs), F32),
            jax.ShapeDtypeStruct((SUBLANES, bsz * seq), F32),
        ],
        scratch_shapes=[
            pltpu.VMEM((2, tq, NB, d), F32),
            pltpu.SemaphoreType.DMA((2,)),
            pltpu.VMEM((POOL_HALO + ts, POOL_WIDTH), F32),
            pltpu.VMEM((CONV_HALO + ts, LRU_WIDTH), F32),
            pltpu.VMEM((ts, LRU_WIDTH), F32),
            pltpu.VMEM((ts, LRU_WIDTH), F32),
            pltpu.VMEM((ts, D_MODEL), F32),
            pltpu.VMEM((ts, D_MODEL), BF16),
            pltpu.VMEM((N_SLABS, ts, MXU_DIM), F32),
            pltpu.VMEM((NB, LRU_WIDTH), F32),
            pltpu.VMEM((RT_ROWS, hs), F32),
        ],
        compiler_params=pltpu.CompilerParams(
            dimension_semantics=("arbitrary", "arbitrary"), vmem_limit_bytes=VMEM_LIMIT),
        name="mixer",
    )(x, *consts)


def _dispatch_kernel(pos_ref, ztile_ref, h2e_ref, xs_ref, zbuf, sem, zsem):
    tb = h2e_ref.shape[0]
    tm = zbuf.shape[0]
    base = pl.program_id(0) * tb

    @pl.when(pl.program_id(0) == 0)
    def _():
        zbuf[...] = jnp.zeros_like(zbuf)

        def zero_copy(c):
            start = pl.multiple_of(ztile_ref[c], SUBLANES)
            return pltpu.make_async_copy(zbuf, xs_ref.at[pl.ds(start, tm)], zsem)

        for c in range(N_CLASSES):
            @pl.when(ztile_ref[c] >= 0)
            def _():
                zero_copy(c).start()
        for c in range(N_CLASSES):
            @pl.when(ztile_ref[c] >= 0)
            def _():
                zero_copy(c).wait()

        def tail_copy(k):
            return pltpu.make_async_copy(zbuf, xs_ref.at[pl.ds(pl.multiple_of(k * tm, SUBLANES), tm)], zsem)

        n_used = ztile_ref[N_CLASSES]
        n_all = xs_ref.shape[0] // tm
        lax.fori_loop(n_used, n_all, lambda k, c: (tail_copy(k).start(), c)[1], 0)
        lax.fori_loop(n_used, n_all, lambda k, c: (tail_copy(k).wait(), c)[1], 0)

    for r in range(tb):
        pltpu.make_async_copy(
            h2e_ref.at[pl.ds(r, 1)], xs_ref.at[pl.ds(pos_ref[base + r], 1)], sem).start(priority=r % 2)
    pltpu.make_async_copy(h2e_ref, xs_ref.at[pl.ds(0, tb)], sem).wait()


def _dispatch(pos, ztile, h2e, n_slots, tm):
    t = h2e.shape[0]
    tb = min(DISPATCH_TILE, t)
    assert t % tb == 0 and n_slots >= tb
    return pl.pallas_call(
        _dispatch_kernel,
        grid_spec=pltpu.PrefetchScalarGridSpec(
            num_scalar_prefetch=2,
            grid=(t // tb,),
            in_specs=[pl.BlockSpec((tb, ROW_W), lambda i, pos, zt: (i, 0))],
            out_specs=pl.BlockSpec(memory_space=pl.ANY),
            scratch_shapes=[pltpu.VMEM((tm, ROW_W), F32),
                            pltpu.SemaphoreType.DMA(()), pltpu.SemaphoreType.DMA(())],
        ),
        out_shape=jax.ShapeDtypeStruct((n_slots, ROW_W), F32),
        compiler_params=pltpu.CompilerParams(
            dimension_semantics=("arbitrary",), vmem_limit_bytes=VMEM_LIMIT),
        name="dispatch",
    )(pos, ztile, h2e)


def _expert_kernel(meta_ref, xs_ref, wg_lo, wu_lo, wd_lo, wg_hi, wu_hi, wd_hi, ys_ref):
    i = pl.program_id(0)

    @pl.when(i < meta_ref[0])
    def _():
        x = xs_ref[:, 0:D_MODEL].astype(BF16)
        w_lo = xs_ref[:, D_MODEL:D_MODEL + 1]
        w_hi = xs_ref[:, D_MODEL + 1:D_MODEL + 2]

        def ffn(wg, wu, wd, w):
            g = _dot(x, wg[0].astype(BF16))
            u = _dot(x, wu[0].astype(BF16))
            act = (g * _sigmoid(g)) * u * w
            return _dot(act.astype(BF16), wd[0].astype(BF16))

        ys_ref[...] = ffn(wg_lo, wu_lo, wd_lo, w_lo) + ffn(wg_hi, wu_hi, wd_hi, w_hi)

    @pl.when(i >= meta_ref[0])
    def _():
        ys_ref[...] = jnp.zeros_like(ys_ref)


def _experts(meta, xs, e_gate, e_up, e_down, n_tiles_max):
    tm = xs.shape[0] // n_tiles_max

    def row_map(i, m):
        return (jnp.minimum(i, m[0] - 1), 0)

    def lo_map(i, m):
        return (m[1 + jnp.minimum(i, m[0] - 1)], 0, 0)

    def hi_map(i, m):
        return (m[1 + n_tiles_max + jnp.minimum(i, m[0] - 1)], 0, 0)

    up_spec = lambda mp: pl.BlockSpec((1, D_MODEL, D_EXPERT), mp)
    down_spec = lambda mp: pl.BlockSpec((1, D_EXPERT, D_MODEL), mp)
    return pl.pallas_call(
        _expert_kernel,
        grid_spec=pltpu.PrefetchScalarGridSpec(
            num_scalar_prefetch=1,
            grid=(n_tiles_max,),
            in_specs=[pl.BlockSpec((tm, ROW_W), row_map),
                      up_spec(lo_map), up_spec(lo_map), down_spec(lo_map),
                      up_spec(hi_map), up_spec(hi_map), down_spec(hi_map)],
            out_specs=pl.BlockSpec((tm, D_MODEL), lambda i, m: (i, 0)),
        ),
        out_shape=jax.ShapeDtypeStruct((xs.shape[0], D_MODEL), F32),
        compiler_params=pltpu.CompilerParams(
            dimension_semantics=("arbitrary",), vmem_limit_bytes=VMEM_LIMIT),
        name="experts",
    )(meta, xs, e_gate, e_up, e_down, e_gate, e_up, e_down)


def _combine_kernel(pos_ref, x1_ref, gf_ref, ys_ref, out_hbm, ybuf, sems, obuf, osems):
    tb = ybuf.shape[1]
    tq = obuf.shape[1]
    nc = out_hbm.shape[1] // tq
    i = pl.program_id(0)
    last = i + 1 == pl.num_programs(0)

    def out_copies(blk, slot):
        g_, c_ = blk // nc, blk % nc
        return [pltpu.make_async_copy(obuf.at[slot, :, bb, :],
                                      out_hbm.at[g_ * NB + bb, pl.ds(c_ * tq, tq), :],
                                      osems.at[slot]) for bb in range(NB)]

    def gather(blk, slot):
        base = blk * tb
        for r in range(tb):
            pltpu.make_async_copy(ys_ref.at[pl.ds(pos_ref[base + r], 1)],
                                  ybuf.at[slot, pl.ds(r, 1)], sems.at[slot]).start(priority=r % 2)

    def finish(blk, slot):
        pltpu.make_async_copy(ys_ref.at[pl.ds(0, tb)], ybuf.at[slot], sems.at[slot]).wait()

        @pl.when(i >= 1)
        def _():
            for cp in out_copies(blk - 2, slot):
                cp.wait()

        rows = slice(slot * tb, (slot + 1) * tb)
        obuf[slot] = _rms_norm(x1_ref[rows, :] + ybuf[slot], gf_ref[...]).reshape(tq, NB, D_MODEL)
        for cp in out_copies(blk, slot):
            cp.start()

        @pl.when(jnp.logical_not(last))
        def _():
            gather(blk + 2, slot)

    @pl.when(i == 0)
    def _():
        gather(0, 0)
        gather(1, 1)

    finish(2 * i, 0)
    finish(2 * i + 1, 1)

    @pl.when(last)
    def _():
        for slot in range(2):
            for cp in out_copies(2 * i + slot, slot):
                cp.wait()


def _combine(pos, x1, gf, ys, bsz, seq):
    t = x1.shape[0]
    tb, tq = _mixer_tiles(bsz, seq)
    assert t % (2 * tb) == 0
    return pl.pallas_call(
        _combine_kernel,
        grid_spec=pltpu.PrefetchScalarGridSpec(
            num_scalar_prefetch=1,
            grid=(t // (2 * tb),),
            in_specs=[pl.BlockSpec((2 * tb, D_MODEL), lambda i, pos: (i, 0)),
                      pl.BlockSpec((1, D_MODEL), lambda i, pos: (0, 0)),
                      pl.BlockSpec(memory_space=pl.ANY)],
            out_specs=pl.BlockSpec(memory_space=pl.ANY),
            scratch_shapes=[pltpu.VMEM((2, tb, D_MODEL), F32), pltpu.SemaphoreType.DMA((2,)),
                            pltpu.VMEM((2, tq, NB, D_MODEL), F32), pltpu.SemaphoreType.DMA((2,))],
        ),
        out_shape=jax.ShapeDtypeStruct((bsz, seq, D_MODEL), F32),
        compiler_params=pltpu.CompilerParams(
            dimension_semantics=("arbitrary",), vmem_limit_bytes=VMEM_LIMIT),
        name="combine",
    )(pos, x1, gf, ys)


def _block_diag(w, per_block):
    n, k, _ = w.shape
    nb = n // per_block
    w = w.reshape(nb, per_block, k, k)
    eye = jnp.eye(per_block, dtype=w.dtype)
    out = jnp.einsum('bpij,pq->bpiqj', w, eye)
    return out.reshape(nb, per_block * k, per_block * k)


def _pair_tables():
    lo, hi = [], []
    for g in range(N_GROUPS):
        for a in range(EXP_PER_GROUP):
            for b in range(a + 1, EXP_PER_GROUP):
                lo.append(g * EXP_PER_GROUP + a)
                hi.append(g * EXP_PER_GROUP + b)
    return jnp.array(lo, jnp.int32), jnp.array(hi, jnp.int32)


def kernel(x, norm1_g, w_in, pool_w, pool_scale, conv_w, conv_b, rg_w_r, rg_b_r, rg_w_i, rg_b_i, rg_lambda, proj_a, proj_b, w_out, norm2_g, router_group_w, router_group_b, router_expert_w, router_expert_b, exp_w_gate, exp_w_up, exp_w_down, norm_f_g):
    bsz, seq, d = x.shape
    t = bsz * seq
    assert w_in.shape[0] == 1, "single-layer block"
    l = 0
    row = lambda v: v.reshape(1, -1).astype(F32)
    heads_per_blk = MXU_DIM // (LRU_WIDTH // rg_w_r.shape[1])
    groups_per_blk = MXU_DIM // POOL_GROUP_DIM
    n_rt = N_GROUPS + N_EXP
    wrt = jnp.concatenate([router_group_w[l].T, router_expert_w[l].T,
                           jnp.zeros((RT_ROWS - n_rt, d), F32)], axis=0)
    brt = jnp.concatenate([router_group_b[l], router_expert_b[l], jnp.zeros((RT_ROWS - n_rt,), F32)])
    hs = _mixer_tiles(bsz, seq)[0] // POST_SPLIT
    brt = jnp.broadcast_to(brt[:, None].astype(F32), (RT_ROWS, hs))

    half = lambda w: (0.5 * w).astype(BF16)
    w_in_b = w_in[l].astype(BF16)
    w_slab = half(jnp.concatenate([w_in[l][:, O_GATE:O_GATE + LRU_WIDTH], w_in[l][:, O_GB:O_GB + D_MODEL]], axis=1))
    w_slab = w_slab.reshape(d, N_SLABS, MXU_DIM).transpose(1, 0, 2)

    x1, h2e, cnt, route = _mixer(
        x, row(norm1_g[l]), jnp.concatenate([w_in_b[:, 0:O_GATE], half(w_in[l][:, O_GA:O_GB])], axis=1),
        _block_diag(pool_w[l], groups_per_blk).astype(BF16),
        row(pool_scale[l]), conv_w[l].astype(F32), row(conv_b[l]),
        half(_block_diag(rg_w_r[l], heads_per_blk)), half(_block_diag(rg_w_i[l], heads_per_blk)),
        0.5 * row(rg_b_r[l]), 0.5 * row(rg_b_i[l]), row(rg_lambda[l]),
        proj_a[l].astype(BF16), proj_b[l].astype(BF16), w_out[l].astype(BF16), row(norm2_g[l]),
        wrt.astype(BF16), brt, w_slab)

    tm = min(EXPERT_TILE, t)
    n_tiles_max = -(-(t + N_CLASSES * (tm - 1)) // tm)
    cls = route[2].astype(jnp.int32)
    rank = route[3].astype(jnp.int32)
    counts = cnt[0:N_CLASSES, 0].astype(jnp.int32)
    padded = ((counts + tm - 1) // tm) * tm
    ends = jnp.cumsum(padded)
    offs = ends - padded
    pos = rank + jnp.sum(jnp.where(cls[:, None] == jnp.arange(N_CLASSES)[None, :], offs[None, :], 0), axis=1)
    n_tiles = ends[-1] // tm
    tile_cls = jnp.minimum(
        jnp.sum((jnp.arange(n_tiles_max)[:, None] * tm >= ends[None, :]).astype(jnp.int32), axis=1),
        N_CLASSES - 1)
    pair_lo, pair_hi = _pair_tables()
    meta = jnp.concatenate([n_tiles[None], pair_lo[tile_cls], pair_hi[tile_cls]]).astype(jnp.int32)

    ztile = jnp.concatenate([jnp.where(padded > 0, ends - tm, -1), n_tiles[None]]).astype(jnp.int32)
    xs = _dispatch(pos, ztile, h2e, n_tiles_max * tm, tm)
    ys = _experts(meta, xs, exp_w_gate[l], exp_w_up[l], exp_w_down[l], n_tiles_max)
    return _combine(pos, x1, row(norm_f_g), ys, bsz, seq)
```

```python
import jax
import jax.numpy as jnp
from jax import lax
from jax.experimental import pallas as pl
from jax.experimental.pallas import tpu as pltpu

F32 = jnp.float32
BF16 = jnp.bfloat16

LANES = 128
SUBLANES = 8
MXU_DIM = 256
NB = SUBLANES

D_MODEL = 1024
POOL_WIDTH = 512
POOL_GROUP_DIM = 128
POOL_WINDOWS = (2, 4, 8, 16)
POOL_HALO = max(POOL_WINDOWS) * NB
LRU_WIDTH = 1024
CONV_WIDTH = 4
CONV_HALO = (CONV_WIDTH - 1) * NB
RG_C = 8.0
LOG2_E = 1.4426950408889634
N_GROUPS = 4
EXP_PER_GROUP = 4
N_EXP = 16
D_EXPERT = 512
EPS = 1e-6
N_PAIRS = 6
N_CLASSES = N_GROUPS * N_PAIRS

META_W = LANES
ROW_W = D_MODEL + META_W
RT_ROWS = 32

O_POOL = 0
O_LRU = POOL_WIDTH
O_GATE = O_LRU + LRU_WIDTH
O_GA = O_GATE + LRU_WIDTH
O_GB = O_GA + D_MODEL
IN_COLS = O_GB + D_MODEL
O_GA_MAIN = O_GATE

SEQ_TILE = 512
POST_SPLIT = 1
N_SLABS = 2 * LRU_WIDTH // MXU_DIM
DISPATCH_TILE = 2048
EXPERT_TILE = 512
VMEM_LIMIT = 56 * 1024 * 1024


def _sigmoid(v):
    return 0.5 * jnp.tanh(0.5 * v) + 0.5


def _sigmoid_of_half(hv):
    return 0.5 * jnp.tanh(hv) + 0.5


def _gelu_tanh_of_half(hv):
    c = 0.7978845608028654
    k0 = 2.0 * c
    k1 = 8.0 * 0.044715 * c
    return hv * (1.0 + jnp.tanh(hv * (k1 * (hv * hv) + k0)))


def _rms_norm(v, g):
    ms = jnp.mean(v * v, axis=-1, keepdims=True)
    return v * lax.rsqrt(ms + EPS) * g


def _dot(a, b):
    return jnp.dot(a, b, preferred_element_type=F32)


def _dot_nt(a, b):
    return lax.dot_general(a, b, (((1,), (1,)), ((), ())), preferred_element_type=F32)


def _first_index_of_max(v, idx, sentinel):
    m = jnp.max(v, axis=0, keepdims=True)
    return m, jnp.min(jnp.where(v == m, idx, sentinel), axis=0, keepdims=True)


def _route(lt, counts):
    n = lt.shape[1]
    sub = lax.broadcasted_iota(jnp.int32, (EXP_PER_GROUP, n), 0).astype(F32)
    none = jnp.float32(EXP_PER_GROUP)
    neg = jnp.float32(-3.0e38)
    gl = lt[0:N_GROUPS, :]
    gmax, gidx = _first_index_of_max(gl, sub, none)
    p_top = 1.0 / jnp.sum(jnp.exp(gl - gmax), axis=0, keepdims=True)
    el = lt[N_GROUPS + (N_GROUPS - 1) * EXP_PER_GROUP:N_GROUPS + N_GROUPS * EXP_PER_GROUP, :]
    for g in range(N_GROUPS - 2, -1, -1):
        lo = N_GROUPS + g * EXP_PER_GROUP
        el = jnp.where(gidx == float(g), lt[lo:lo + EXP_PER_GROUP, :], el)
    v1, i1 = _first_index_of_max(el, sub, none)
    v2, i2 = _first_index_of_max(jnp.where(sub == i1, neg, el), sub, none)
    ex = jnp.exp(v2 - v1)
    w1 = p_top / (1.0 + ex)
    w2 = p_top * ex / (1.0 + ex)
    first_lo = i1 < i2
    w_lo = jnp.where(first_lo, w1, w2)
    w_hi = jnp.where(first_lo, w2, w1)
    pa_ = jnp.minimum(i1, i2)
    pb_ = jnp.maximum(i1, i2)
    cls = gidx * N_PAIRS + pa_ * (7.0 - pa_) * 0.5 + (pb_ - pa_ - 1.0)

    crow = lax.broadcasted_iota(jnp.int32, (RT_ROWS, n), 0).astype(F32)
    onehot = crow == cls
    t_row = lax.broadcasted_iota(jnp.int32, (n, n), 0)
    t_col = lax.broadcasted_iota(jnp.int32, (n, n), 1)
    earlier = jnp.where(t_row < t_col, 1.0, 0.0).astype(BF16)
    prior = _dot(jnp.where(onehot, 1.0, 0.0).astype(BF16), earlier) + counts
    rank = jnp.sum(jnp.where(onehot, prior, 0.0), axis=0, keepdims=True)
    counts = counts + jnp.sum(jnp.where(onehot, 1.0, 0.0), axis=1, keepdims=True)

    mrow = lax.broadcasted_iota(jnp.int32, (SUBLANES, n), 0)
    meta = jnp.where(mrow == 0, w_lo,
                     jnp.where(mrow == 1, w_hi,
                               jnp.where(mrow == 2, cls,
                                         jnp.where(mrow == 3, rank, 0.0))))
    return meta, counts


def _mixer_kernel(x_hbm, g1_ref, w_in_ref, pool_bd_ref, pool_scale_ref, conv_w_ref, conv_b_ref,
                  wr_ref, wi_ref, br_ref, bi_ref, lam_ref, pa_ref, pb_ref, wo_ref, g2_ref,
                  wrt_ref, brt_ref, w_slab_ref,
                  x1_ref, h2e_ref, cnt_ref, route_ref,
                  xbuf, xsem, pool_ext, lru_ext, a_buf, b_buf, mixa_buf, h_buf, zg_buf, h_carry, cnt_scr):
    g = pl.program_id(0)
    c = pl.program_id(1)
    nc = pl.num_programs(1)
    tq = xbuf.shape[1]
    ts = tq * NB
    step = g * nc + c
    slot = step % 2

    def x_copies(g_, c_, slot_):
        return [pltpu.make_async_copy(x_hbm.at[g_ * NB + bb, pl.ds(c_ * tq, tq), :],
                                      xbuf.at[slot_, :, bb, :], xsem.at[slot_]) for bb in range(NB)]

    @pl.when(step == 0)
    def _():
        for cp in x_copies(g, c, slot):
            cp.start()

    @pl.when(step + 1 < pl.num_programs(0) * nc)
    def _():
        wrap = c + 1 == nc
        for cp in x_copies(jnp.where(wrap, g + 1, g), jnp.where(wrap, 0, c + 1), 1 - slot):
            cp.start()

    @pl.when(c == 0)
    def _():
        pool_ext[0:POOL_HALO, :] = jnp.zeros((POOL_HALO, POOL_WIDTH), F32)
        lru_ext[0:CONV_HALO, :] = jnp.zeros((CONV_HALO, LRU_WIDTH), F32)
        h_carry[...] = jnp.zeros_like(h_carry)

    @pl.when(step == 0)
    def _():
        cnt_scr[...] = jnp.zeros_like(cnt_scr)

    for cp in x_copies(g, c, slot):
        cp.wait()
    x = xbuf[slot].reshape(ts, D_MODEL)
    h = _rms_norm(x, g1_ref[...]).astype(BF16)
    h_buf[...] = h

    pool_ext[POOL_HALO:POOL_HALO + ts, :] = _dot(h, w_in_ref[:, O_POOL:O_POOL + POOL_WIDTH])
    ext = pool_ext[...]
    pool_ext[0:POOL_HALO, :] = ext[ts:ts + POOL_HALO, :]
    t_idx = lax.broadcasted_iota(jnp.int32, (ts, POOL_GROUP_DIM), 0) // NB
    pos1 = (c * tq + 1 + t_idx).astype(F32)
    zs = []
    for gi, w in enumerate(POOL_WINDOWS):
        s = ext[:, gi * POOL_GROUP_DIM:(gi + 1) * POOL_GROUP_DIM]
        u = s[POOL_HALO:]
        d = 1
        while d < w:
            s = s[d * NB:] + s[:-d * NB]
            d *= 2
        off = POOL_HALO - (w - 1) * NB
        win = s[off:off + ts]
        zs.append(win / jnp.minimum(pos1, float(w)) - u)
    z = jnp.concatenate(zs, axis=1).astype(BF16)
    y_pool = jnp.concatenate(
        [_dot(z[:, j * MXU_DIM:(j + 1) * MXU_DIM], pool_bd_ref[j]) for j in range(POOL_WIDTH // MXU_DIM)],
        axis=1) * pool_scale_ref[...]
    y_a = _dot(y_pool.astype(BF16), pa_ref[...])
    mixa_buf[...] = _sigmoid_of_half(_dot(h, w_in_ref[:, O_GA_MAIN:O_GA_MAIN + D_MODEL])) * y_a

    lru_ext[CONV_HALO:CONV_HALO + ts, :] = _dot(h, w_in_ref[:, O_LRU:O_LRU + LRU_WIDTH])
    xc = conv_b_ref[...] + conv_w_ref[CONV_WIDTH - 1:CONV_WIDTH, :] * lru_ext[CONV_HALO:CONV_HALO + ts, :]
    for k in range(CONV_WIDTH - 1):
        o = k * NB
        xc = xc + conv_w_ref[k:k + 1, :] * lru_ext[o:o + ts, :]
    lru_ext[0:CONV_HALO, :] = lru_ext[ts:ts + CONV_HALO, :]
    xcb = xc.astype(BF16)
    nblk = LRU_WIDTH // MXU_DIM
    t_r = jnp.tanh(jnp.concatenate(
        [_dot(xcb[:, j * MXU_DIM:(j + 1) * MXU_DIM], wr_ref[j]) for j in range(nblk)], axis=1) + br_ref[...])
    i = _sigmoid_of_half(jnp.concatenate(
        [_dot(xcb[:, j * MXU_DIM:(j + 1) * MXU_DIM], wi_ref[j]) for j in range(nblk)], axis=1) + bi_ref[...])
    nlam = -lam_ref[...]
    softplus = jnp.maximum(nlam, 0.0) + jnp.log1p(jnp.exp(-jnp.abs(nlam)))
    half_c = (0.5 * RG_C) * softplus
    y = half_c * t_r + half_c
    a = jnp.exp2(y * (-LOG2_E))
    a_buf[...] = a
    v = jnp.tanh(y) * (1.0 + a * a)
    b_buf[...] = jnp.where(v > 0.0, v * lax.rsqrt(v), 0.0) * (i * xc)

    steps_per_trip = tq // N_SLABS
    carry = h_carry[...]
    for j in range(N_SLABS):
        zg_buf[j] = _dot(h_buf[...], w_slab_ref[j])
        for q in range(steps_per_trip):
            r0 = (j * steps_per_trip + q) * NB
            carry = b_buf[r0:r0 + NB, :] + a_buf[r0:r0 + NB, :] * carry
            b_buf[r0:r0 + NB, :] = carry
    h_carry[...] = carry

    hs = ts // POST_SPLIT
    counts = cnt_scr[...]
    for part in range(POST_SPLIT):
        rows = slice(part * hs, (part + 1) * hs)
        half = N_SLABS // 2
        z_gate = jnp.concatenate([zg_buf[j, rows, :] for j in range(half)], axis=1)
        z_gb = jnp.concatenate([zg_buf[half + j, rows, :] for j in range(half)], axis=1)
        y_lru = b_buf[rows, :] * _gelu_tanh_of_half(z_gate)
        y_b = _dot(y_lru.astype(BF16), pb_ref[...])
        mix = mixa_buf[rows, :] + _sigmoid_of_half(z_gb) * y_b
        x1 = x[rows, :] + _dot(mix.astype(BF16), wo_ref[...])
        x1_ref[rows, :] = x1
        h2 = _rms_norm(x1, g2_ref[...])
        h2e_ref[rows, 0:D_MODEL] = h2
        lt = _dot_nt(wrt_ref[...], h2.astype(BF16)) + brt_ref[...]
        meta, counts = _route(lt, counts)
        route_ref[:, rows] = meta
        meta_t = jnp.concatenate([meta, jnp.zeros((META_W - SUBLANES, hs), F32)], axis=0).T
        h2e_ref[rows, D_MODEL:ROW_W] = meta_t
    cnt_scr[...] = counts
    cnt_ref[...] = counts


def _const_spec(shape):
    zeros = (0,) * len(shape)
    return pl.BlockSpec(shape, lambda b, c: zeros, pipeline_mode=pl.Buffered(1))


def _mixer_tiles(bsz, seq):
    tq = min(SEQ_TILE // NB, seq)
    ts = tq * NB
    assert bsz % NB == 0 and seq % tq == 0 and tq % N_SLABS == 0 and ts >= POOL_HALO
    return ts, tq


def _mixer(x, g1, w_in, pool_bd, pool_scale, conv_w, conv_b, wr4, wi4, b_r, b_i, lam, pa, pb, wo, g2,
           wrt, brt, w_slab):
    bsz, seq, d = x.shape
    ts, tq = _mixer_tiles(bsz, seq)
    hs = ts // POST_SPLIT
    assert d == D_MODEL and hs % LANES == 0 and brt.shape == (RT_ROWS, hs)
    nc = seq // tq
    consts = (g1, w_in, pool_bd, pool_scale, conv_w, conv_b, wr4, wi4, b_r, b_i, lam, pa, pb, wo, g2,
              wrt, brt, w_slab)
    return pl.pallas_call(
        _mixer_kernel,
        grid=(bsz // NB, nc),
        in_specs=[pl.BlockSpec(memory_space=pl.ANY)] + [_const_spec(a.shape) for a in consts],
        out_specs=[
            pl.BlockSpec((ts, d), lambda g, c: (g * nc + c, 0)),
            pl.BlockSpec((ts, ROW_W), lambda g, c: (g * nc + c, 0)),
            pl.BlockSpec((RT_ROWS, hs), lambda g, c: (0, 0)),
            pl.BlockSpec((SUBLANES, ts), lambda g, c: (0, g * nc + c)),
        ],
        out_shape=[
            jax.ShapeDtypeStruct((bsz * seq, d), F32),
            jax.ShapeDtypeStruct((bsz * seq, ROW_W), F32),
            jax.ShapeDtypeStruct((RT_ROWS, hs), F32),
            jax.ShapeDtypeStruct((SUBLANES, bsz * seq), F32),
        ],
        scratch_shapes=[
            pltpu.VMEM((2, tq, NB, d), F32),
            pltpu.SemaphoreType.DMA((2,)),
            pltpu.VMEM((POOL_HALO + ts, POOL_WIDTH), F32),
            pltpu.VMEM((CONV_HALO + ts, LRU_WIDTH), F32),
            pltpu.VMEM((ts, LRU_WIDTH), F32),
            pltpu.VMEM((ts, LRU_WIDTH), F32),
            pltpu.VMEM((ts, D_MODEL), F32),
            pltpu.VMEM((ts, D_MODEL), BF16),
            pltpu.VMEM((N_SLABS, ts, MXU_DIM), F32),
            pltpu.VMEM((NB, LRU_WIDTH), F32),
            pltpu.VMEM((RT_ROWS, hs), F32),
        ],
        compiler_params=pltpu.CompilerParams(
            dimension_semantics=("arbitrary", "arbitrary"), vmem_limit_bytes=VMEM_LIMIT),
        name="mixer",
    )(x, *consts)


def _dispatch_kernel(pos_ref, ztile_ref, h2e_ref, xs_ref, zbuf, sem, zsem):
    tb = h2e_ref.shape[0]
    tm = zbuf.shape[0]
    base = pl.program_id(0) * tb

    @pl.when(pl.program_id(0) == 0)
    def _():
        zbuf[...] = jnp.zeros_like(zbuf)

        def zero_copy(c):
            start = pl.multiple_of(ztile_ref[c], SUBLANES)
            return pltpu.make_async_copy(zbuf, xs_ref.at[pl.ds(start, tm)], zsem)

        for c in range(N_CLASSES):
            @pl.when(ztile_ref[c] >= 0)
            def _():
                zero_copy(c).start()
        for c in range(N_CLASSES):
            @pl.when(ztile_ref[c] >= 0)
            def _():
                zero_copy(c).wait()

        def tail_copy(k):
            return pltpu.make_async_copy(zbuf, xs_ref.at[pl.ds(pl.multiple_of(k * tm, SUBLANES), tm)], zsem)

        n_used = ztile_ref[N_CLASSES]
        n_all = xs_ref.shape[0] // tm
        lax.fori_loop(n_used, n_all, lambda k, c: (tail_copy(k).start(), c)[1], 0)
        lax.fori_loop(n_used, n_all, lambda k, c: (tail_copy(k).wait(), c)[1], 0)

    for r in range(tb):
        pltpu.make_async_copy(
            h2e_ref.at[pl.ds(r, 1)], xs_ref.at[pl.ds(pos_ref[base + r], 1)], sem).start(priority=r % 2)
    pltpu.make_async_copy(h2e_ref, xs_ref.at[pl.ds(0, tb)], sem).wait()


def _dispatch(pos, ztile, h2e, n_slots, tm):
    t = h2e.shape[0]
    tb = min(DISPATCH_TILE, t)
    assert t % tb == 0 and n_slots >= tb
    return pl.pallas_call(
        _dispatch_kernel,
        grid_spec=pltpu.PrefetchScalarGridSpec(
            num_scalar_prefetch=2,
            grid=(t // tb,),
            in_specs=[pl.BlockSpec((tb, ROW_W), lambda i, pos, zt: (i, 0))],
            out_specs=pl.BlockSpec(memory_space=pl.ANY),
            scratch_shapes=[pltpu.VMEM((tm, ROW_W), F32),
                            pltpu.SemaphoreType.DMA(()), pltpu.SemaphoreType.DMA(())],
        ),
        out_shape=jax.ShapeDtypeStruct((n_slots, ROW_W), F32),
        compiler_params=pltpu.CompilerParams(
            dimension_semantics=("arbitrary",), vmem_limit_bytes=VMEM_LIMIT),
        name="dispatch",
    )(pos, ztile, h2e)


def _expert_kernel(meta_ref, xs_ref, wg_lo, wu_lo, wd_lo, wg_hi, wu_hi, wd_hi, ys_ref):
    i = pl.program_id(0)

    @pl.when(i < meta_ref[0])
    def _():
        x = xs_ref[:, 0:D_MODEL].astype(BF16)
        w_lo = xs_ref[:, D_MODEL:D_MODEL + 1]
        w_hi = xs_ref[:, D_MODEL + 1:D_MODEL + 2]

        def ffn(wg, wu, wd, w):
            g = _dot(x, wg[0].astype(BF16))
            u = _dot(x, wu[0].astype(BF16))
            act = (g * _sigmoid(g)) * u * w
            return _dot(act.astype(BF16), wd[0].astype(BF16))

        ys_ref[...] = ffn(wg_lo, wu_lo, wd_lo, w_lo) + ffn(wg_hi, wu_hi, wd_hi, w_hi)

    @pl.when(i >= meta_ref[0])
    def _():
        ys_ref[...] = jnp.zeros_like(ys_ref)


def _experts(meta, xs, e_gate, e_up, e_down, n_tiles_max):
    tm = xs.shape[0] // n_tiles_max

    def row_map(i, m):
        return (jnp.minimum(i, m[0] - 1), 0)

    def lo_map(i, m):
        return (m[1 + jnp.minimum(i, m[0] - 1)], 0, 0)

    def hi_map(i, m):
        return (m[1 + n_tiles_max + jnp.minimum(i, m[0] - 1)], 0, 0)

    up_spec = lambda mp: pl.BlockSpec((1, D_MODEL, D_EXPERT), mp)
    down_spec = lambda mp: pl.BlockSpec((1, D_EXPERT, D_MODEL), mp)
    return pl.pallas_call(
        _expert_kernel,
        grid_spec=pltpu.PrefetchScalarGridSpec(
            num_scalar_prefetch=1,
            grid=(n_tiles_max,),
            in_specs=[pl.BlockSpec((tm, ROW_W), row_map),
                      up_spec(lo_map), up_spec(lo_map), down_spec(lo_map),
                      up_spec(hi_map), up_spec(hi_map), down_spec(hi_map)],
            out_specs=pl.BlockSpec((tm, D_MODEL), lambda i, m: (i, 0)),
        ),
        out_shape=jax.ShapeDtypeStruct((xs.shape[0], D_MODEL), F32),
        compiler_params=pltpu.CompilerParams(
            dimension_semantics=("arbitrary",), vmem_limit_bytes=VMEM_LIMIT),
        name="experts",
    )(meta, xs, e_gate, e_up, e_down, e_gate, e_up, e_down)


def _combine_kernel(pos_ref, x1_ref, gf_ref, ys_ref, out_hbm, ybuf, sems, obuf, osems):
    tb = ybuf.shape[1]
    tq = obuf.shape[1]
    nc = out_hbm.shape[1] // tq
    i = pl.program_id(0)
    last = i + 1 == pl.num_programs(0)

    def out_copies(blk, slot):
        g_, c_ = blk // nc, blk % nc
        return [pltpu.make_async_copy(obuf.at[slot, :, bb, :],
                                      out_hbm.at[g_ * NB + bb, pl.ds(c_ * tq, tq), :],
                                      osems.at[slot]) for bb in range(NB)]

    def gather(blk, slot):
        base = blk * tb
        for r in range(tb):
            pltpu.make_async_copy(ys_ref.at[pl.ds(pos_ref[base + r], 1)],
                                  ybuf.at[slot, pl.ds(r, 1)], sems.at[slot]).start(priority=r % 2)

    def finish(blk, slot):
        pltpu.make_async_copy(ys_ref.at[pl.ds(0, tb)], ybuf.at[slot], sems.at[slot]).wait()

        @pl.when(i >= 1)
        def _():
            for cp in out_copies(blk - 2, slot):
                cp.wait()

        rows = slice(slot * tb, (slot + 1) * tb)
        obuf[slot] = _rms_norm(x1_ref[rows, :] + ybuf[slot], gf_ref[...]).reshape(tq, NB, D_MODEL)
        for cp in out_copies(blk, slot):
            cp.start()

        @pl.when(jnp.logical_not(last))
        def _():
            gather(blk + 2, slot)

    @pl.when(i == 0)
    def _():
        gather(0, 0)
        gather(1, 1)

    finish(2 * i, 0)
    finish(2 * i + 1, 1)

    @pl.when(last)
    def _():
        for slot in range(2):
            for cp in out_copies(2 * i + slot, slot):
                cp.wait()


def _combine(pos, x1, gf, ys, bsz, seq):
    t = x1.shape[0]
    tb, tq = _mixer_tiles(bsz, seq)
    assert t % (2 * tb) == 0
    return pl.pallas_call(
        _combine_kernel,
        grid_spec=pltpu.PrefetchScalarGridSpec(
            num_scalar_prefetch=1,
            grid=(t // (2 * tb),),
            in_specs=[pl.BlockSpec((2 * tb, D_MODEL), lambda i, pos: (i, 0)),
                      pl.BlockSpec((1, D_MODEL), lambda i, pos: (0, 0)),
                      pl.BlockSpec(memory_space=pl.ANY)],
            out_specs=pl.BlockSpec(memory_space=pl.ANY),
            scratch_shapes=[pltpu.VMEM((2, tb, D_MODEL), F32), pltpu.SemaphoreType.DMA((2,)),
                            pltpu.VMEM((2, tq, NB, D_MODEL), F32), pltpu.SemaphoreType.DMA((2,))],
        ),
        out_shape=jax.ShapeDtypeStruct((bsz, seq, D_MODEL), F32),
        compiler_params=pltpu.CompilerParams(
            dimension_semantics=("arbitrary",), vmem_limit_bytes=VMEM_LIMIT),
        name="combine",
    )(pos, x1, gf, ys)


def _block_diag(w, per_block):
    n, k, _ = w.shape
    nb = n // per_block
    w = w.reshape(nb, per_block, k, k)
    eye = jnp.eye(per_block, dtype=w.dtype)
    out = jnp.einsum('bpij,pq->bpiqj', w, eye)
    return out.reshape(nb, per_block * k, per_block * k)


def _pair_tables():
    lo, hi = [], []
    for g in range(N_GROUPS):
        for a in range(EXP_PER_GROUP):
            for b in range(a + 1, EXP_PER_GROUP):
                lo.append(g * EXP_PER_GROUP + a)
                hi.append(g * EXP_PER_GROUP + b)
    return jnp.array(lo, jnp.int32), jnp.array(hi, jnp.int32)


def kernel(x, norm1_g, w_in, pool_w, pool_scale, conv_w, conv_b, rg_w_r, rg_b_r, rg_w_i, rg_b_i, rg_lambda, proj_a, proj_b, w_out, norm2_g, router_group_w, router_group_b, router_expert_w, router_expert_b, exp_w_gate, exp_w_up, exp_w_down, norm_f_g):
    bsz, seq, d = x.shape
    t = bsz * seq
    assert w_in.shape[0] == 1, "single-layer block"
    l = 0
    row = lambda v: v.reshape(1, -1).astype(F32)
    heads_per_blk = MXU_DIM // (LRU_WIDTH // rg_w_r.shape[1])
    groups_per_blk = MXU_DIM // POOL_GROUP_DIM
    n_rt = N_GROUPS + N_EXP
    wrt = jnp.concatenate([router_group_w[l].T, router_expert_w[l].T,
                           jnp.zeros((RT_ROWS - n_rt, d), F32)], axis=0)
    brt = jnp.concatenate([router_group_b[l], router_expert_b[l], jnp.zeros((RT_ROWS - n_rt,), F32)])
    hs = _mixer_tiles(bsz, seq)[0] // POST_SPLIT
    brt = jnp.broadcast_to(brt[:, None].astype(F32), (RT_ROWS, hs))

    half = lambda w: (0.5 * w).astype(BF16)
    w_in_b = w_in[l].astype(BF16)
    half_b = jnp.asarray(0.5, BF16)
    w_slab = jnp.concatenate([w_in_b[:, O_GATE:O_GATE + LRU_WIDTH], w_in_b[:, O_GB:O_GB + D_MODEL]], axis=1) * half_b
    w_slab = w_slab.reshape(d, N_SLABS, MXU_DIM).transpose(1, 0, 2)

    x1, h2e, cnt, route = _mixer(
        x, row(norm1_g[l]), jnp.concatenate([w_in_b[:, 0:O_GATE], w_in_b[:, O_GA:O_GB] * half_b], axis=1),
        _block_diag(pool_w[l], groups_per_blk).astype(BF16),
        row(pool_scale[l]), conv_w[l].astype(F32), row(conv_b[l]),
        half(_block_diag(rg_w_r[l], heads_per_blk)), half(_block_diag(rg_w_i[l], heads_per_blk)),
        0.5 * row(rg_b_r[l]), 0.5 * row(rg_b_i[l]), row(rg_lambda[l]),
        proj_a[l].astype(BF16), proj_b[l].astype(BF16), w_out[l].astype(BF16), row(norm2_g[l]),
        wrt.astype(BF16), brt, w_slab)

    tm = min(EXPERT_TILE, t)
    n_tiles_max = -(-(t + N_CLASSES * (tm - 1)) // tm)
    cls = route[2].astype(jnp.int32)
    rank = route[3].astype(jnp.int32)
    counts = cnt[0:N_CLASSES, 0].astype(jnp.int32)
    padded = ((counts + tm - 1) // tm) * tm
    ends = jnp.cumsum(padded)
    offs = ends - padded
    pos = rank + jnp.sum(jnp.where(cls[:, None] == jnp.arange(N_CLASSES)[None, :], offs[None, :], 0), axis=1)
    n_tiles = ends[-1] // tm
    tile_cls = jnp.minimum(
        jnp.sum((jnp.arange(n_tiles_max)[:, None] * tm >= ends[None, :]).astype(jnp.int32), axis=1),
        N_CLASSES - 1)
    pair_lo, pair_hi = _pair_tables()
    meta = jnp.concatenate([n_tiles[None], pair_lo[tile_cls], pair_hi[tile_cls]]).astype(jnp.int32)

    ztile = jnp.concatenate([jnp.where(padded > 0, ends - tm, -1), n_tiles[None]]).astype(jnp.int32)
    xs = _dispatch(pos, ztile, h2e, n_tiles_max * tm, tm)
    ys = _experts(meta, xs, exp_w_gate[l], exp_w_up[l], exp_w_down[l], n_tiles_max)
    return _combine(pos, x1, row(norm_f_g), ys, bsz, seq)
```

```python
import jax
import jax.numpy as jnp
from jax import lax
from jax.experimental import pallas as pl
from jax.experimental.pallas import tpu as pltpu

F32 = jnp.float32
BF16 = jnp.bfloat16

LANES = 128
SUBLANES = 8
MXU_DIM = 256
NB = SUBLANES

D_MODEL = 1024
POOL_WIDTH = 512
POOL_GROUP_DIM = 128
POOL_WINDOWS = (2, 4, 8, 16)
POOL_HALO = max(POOL_WINDOWS) * NB
LRU_WIDTH = 1024
CONV_WIDTH = 4
CONV_HALO = (CONV_WIDTH - 1) * NB
RG_C = 8.0
N_GROUPS = 4
EXP_PER_GROUP = 4
N_EXP = 16
D_EXPERT = 512
EPS = 1e-6
N_PAIRS = 6
N_CLASSES = N_GROUPS * N_PAIRS

META_W = LANES
ROW_W = D_MODEL + META_W
RT_ROWS = 32

O_POOL = 0
O_LRU = POOL_WIDTH
O_GATE = O_LRU + LRU_WIDTH
O_GA = O_GATE + LRU_WIDTH
O_GB = O_GA + D_MODEL
IN_COLS = O_GB + D_MODEL
O_GA_MAIN = O_GATE

SEQ_TILE = 512
POST_SPLIT = 1
N_SLABS = 2 * LRU_WIDTH // MXU_DIM
DISPATCH_TILE = 2048
EXPERT_TILE = 512
VMEM_LIMIT = 56 * 1024 * 1024


def _sigmoid(v):
    return 0.5 * jnp.tanh(0.5 * v) + 0.5


def _gelu_tanh(v):
    c = 0.7978845608028654
    return 0.5 * v * (1.0 + jnp.tanh(c * (v + 0.044715 * (v * v * v))))


def _rms_norm(v, g):
    ms = jnp.mean(v * v, axis=-1, keepdims=True)
    return v * lax.rsqrt(ms + EPS) * g


def _dot(a, b):
    return jnp.dot(a, b, preferred_element_type=F32)


def _dot_nt(a, b):
    return lax.dot_general(a, b, (((1,), (1,)), ((), ())), preferred_element_type=F32)


def _first_index_of_max(v, idx, sentinel):
    m = jnp.max(v, axis=0, keepdims=True)
    return m, jnp.min(jnp.where(v == m, idx, sentinel), axis=0, keepdims=True)


def _route(lt, counts):
    n = lt.shape[1]
    sub = lax.broadcasted_iota(jnp.int32, (EXP_PER_GROUP, n), 0).astype(F32)
    none = jnp.float32(EXP_PER_GROUP)
    neg = jnp.float32(-3.0e38)
    gl = lt[0:N_GROUPS, :]
    gmax, gidx = _first_index_of_max(gl, sub, none)
    p_top = 1.0 / jnp.sum(jnp.exp(gl - gmax), axis=0, keepdims=True)
    el = lt[N_GROUPS + (N_GROUPS - 1) * EXP_PER_GROUP:N_GROUPS + N_GROUPS * EXP_PER_GROUP, :]
    for g in range(N_GROUPS - 2, -1, -1):
        lo = N_GROUPS + g * EXP_PER_GROUP
        el = jnp.where(gidx == float(g), lt[lo:lo + EXP_PER_GROUP, :], el)
    v1, i1 = _first_index_of_max(el, sub, none)
    v2, i2 = _first_index_of_max(jnp.where(sub == i1, neg, el), sub, none)
    ex = jnp.exp(v2 - v1)
    w1 = p_top / (1.0 + ex)
    w2 = p_top * ex / (1.0 + ex)
    first_lo = i1 < i2
    w_lo = jnp.where(first_lo, w1, w2)
    w_hi = jnp.where(first_lo, w2, w1)
    pa_ = jnp.minimum(i1, i2)
    pb_ = jnp.maximum(i1, i2)
    cls = gidx * N_PAIRS + pa_ * (7.0 - pa_) * 0.5 + (pb_ - pa_ - 1.0)

    crow = lax.broadcasted_iota(jnp.int32, (RT_ROWS, n), 0).astype(F32)
    onehot = crow == cls
    t_row = lax.broadcasted_iota(jnp.int32, (n, n), 0)
    t_col = lax.broadcasted_iota(jnp.int32, (n, n), 1)
    earlier = jnp.where(t_row < t_col, 1.0, 0.0).astype(BF16)
    prior = _dot(jnp.where(onehot, 1.0, 0.0).astype(BF16), earlier) + counts
    rank = jnp.sum(jnp.where(onehot, prior, 0.0), axis=0, keepdims=True)
    counts = counts + jnp.sum(jnp.where(onehot, 1.0, 0.0), axis=1, keepdims=True)

    mrow = lax.broadcasted_iota(jnp.int32, (SUBLANES, n), 0)
    meta = jnp.where(mrow == 0, w_lo,
                     jnp.where(mrow == 1, w_hi,
                               jnp.where(mrow == 2, cls,
                                         jnp.where(mrow == 3, rank, 0.0))))
    return meta, counts


def _mixer_kernel(x_hbm, g1_ref, w_in_ref, pool_bd_ref, pool_scale_ref, conv_w_ref, conv_b_ref,
                  wr_ref, wi_ref, br_ref, bi_ref, lam_ref, pa_ref, pb_ref, wo_ref, g2_ref,
                  wrt_ref, brt_ref, w_slab_ref,
                  x1_ref, h2e_ref, cnt_ref, route_ref,
                  xbuf, xsem, pool_ext, lru_ext, a_buf, b_buf, mixa_buf, h_buf, zg_buf, h_carry, cnt_scr):
    g = pl.program_id(0)
    c = pl.program_id(1)
    nc = pl.num_programs(1)
    tq = xbuf.shape[1]
    ts = tq * NB
    step = g * nc + c
    slot = step % 2

    def x_copies(g_, c_, slot_):
        return [pltpu.make_async_copy(x_hbm.at[g_ * NB + bb, pl.ds(c_ * tq, tq), :],
                                      xbuf.at[slot_, :, bb, :], xsem.at[slot_]) for bb in range(NB)]

    @pl.when(step == 0)
    def _():
        for cp in x_copies(g, c, slot):
            cp.start()

    @pl.when(step + 1 < pl.num_programs(0) * nc)
    def _():
        wrap = c + 1 == nc
        for cp in x_copies(jnp.where(wrap, g + 1, g), jnp.where(wrap, 0, c + 1), 1 - slot):
            cp.start()

    @pl.when(c == 0)
    def _():
        pool_ext[0:POOL_HALO, :] = jnp.zeros((POOL_HALO, POOL_WIDTH), F32)
        lru_ext[0:CONV_HALO, :] = jnp.zeros((CONV_HALO, LRU_WIDTH), F32)
        h_carry[...] = jnp.zeros_like(h_carry)

    @pl.when(step == 0)
    def _():
        cnt_scr[...] = jnp.zeros_like(cnt_scr)

    for cp in x_copies(g, c, slot):
        cp.wait()
    x = xbuf[slot].reshape(ts, D_MODEL)
    h = _rms_norm(x, g1_ref[...]).astype(BF16)
    h_buf[...] = h

    pool_ext[POOL_HALO:POOL_HALO + ts, :] = _dot(h, w_in_ref[:, O_POOL:O_POOL + POOL_WIDTH])
    ext = pool_ext[...]
    pool_ext[0:POOL_HALO, :] = ext[ts:ts + POOL_HALO, :]
    t_idx = lax.broadcasted_iota(jnp.int32, (ts, POOL_GROUP_DIM), 0) // NB
    pos1 = (c * tq + 1 + t_idx).astype(F32)
    zs = []
    for gi, w in enumerate(POOL_WINDOWS):
        s = ext[:, gi * POOL_GROUP_DIM:(gi + 1) * POOL_GROUP_DIM]
        u = s[POOL_HALO:]
        d = 1
        while d < w:
            s = s[d * NB:] + s[:-d * NB]
            d *= 2
        off = POOL_HALO - (w - 1) * NB
        win = s[off:off + ts]
        zs.append(win / jnp.minimum(pos1, float(w)) - u)
    z = jnp.concatenate(zs, axis=1).astype(BF16)
    y_pool = jnp.concatenate(
        [_dot(z[:, j * MXU_DIM:(j + 1) * MXU_DIM], pool_bd_ref[j]) for j in range(POOL_WIDTH // MXU_DIM)],
        axis=1) * pool_scale_ref[...]
    y_a = _dot(y_pool.astype(BF16), pa_ref[...])
    mixa_buf[...] = _sigmoid(_dot(h, w_in_ref[:, O_GA_MAIN:O_GA_MAIN + D_MODEL])) * y_a

    lru_ext[CONV_HALO:CONV_HALO + ts, :] = _dot(h, w_in_ref[:, O_LRU:O_LRU + LRU_WIDTH])
    nlam = -lam_ref[...]
    softplus = jnp.maximum(nlam, 0.0) + jnp.log1p(jnp.exp(-jnp.abs(nlam)))
    decay = -RG_C * softplus
    for j in range(LRU_WIDTH // MXU_DIM):
        cols = slice(j * MXU_DIM, (j + 1) * MXU_DIM)
        xc = conv_b_ref[:, cols] + conv_w_ref[CONV_WIDTH - 1:CONV_WIDTH, cols] * lru_ext[CONV_HALO:CONV_HALO + ts, cols]
        for k in range(CONV_WIDTH - 1):
            o = k * NB
            xc = xc + conv_w_ref[k:k + 1, cols] * lru_ext[o:o + ts, cols]
        xcb = xc.astype(BF16)
        r = _sigmoid(_dot(xcb, wr_ref[j]) + br_ref[:, cols])
        i = _sigmoid(_dot(xcb, wi_ref[j]) + bi_ref[:, cols])
        log_a = decay[:, cols] * r
        a = jnp.exp(log_a)
        a_buf[:, cols] = a
        v = jnp.tanh(-log_a) * (1.0 + a * a)
        b_buf[:, cols] = jnp.where(v > 0.0, v * lax.rsqrt(v), 0.0) * (i * xc)
    lru_ext[0:CONV_HALO, :] = lru_ext[ts:ts + CONV_HALO, :]

    steps_per_trip = tq // N_SLABS
    carry = h_carry[...]
    for j in range(N_SLABS):
        zg_buf[j] = _dot(h_buf[...], w_slab_ref[j])
        for q in range(steps_per_trip):
            r0 = (j * steps_per_trip + q) * NB
            carry = b_buf[r0:r0 + NB, :] + a_buf[r0:r0 + NB, :] * carry
            b_buf[r0:r0 + NB, :] = carry
    h_carry[...] = carry

    hs = ts // POST_SPLIT
    counts = cnt_scr[...]
    for part in range(POST_SPLIT):
        rows = slice(part * hs, (part + 1) * hs)
        half = N_SLABS // 2
        z_gate = jnp.concatenate([zg_buf[j, rows, :] for j in range(half)], axis=1)
        z_gb = jnp.concatenate([zg_buf[half + j, rows, :] for j in range(half)], axis=1)
        y_lru = b_buf[rows, :] * _gelu_tanh(z_gate)
        y_b = _dot(y_lru.astype(BF16), pb_ref[...])
        mix = mixa_buf[rows, :] + _sigmoid(z_gb) * y_b
        x1 = x[rows, :] + _dot(mix.astype(BF16), wo_ref[...])
        x1_ref[rows, :] = x1
        h2 = _rms_norm(x1, g2_ref[...])
        h2e_ref[rows, 0:D_MODEL] = h2
        lt = _dot_nt(wrt_ref[...], h2.astype(BF16)) + brt_ref[...]
        meta, counts = _route(lt, counts)
        route_ref[:, rows] = meta
        meta_t = jnp.concatenate([meta, jnp.zeros((META_W - SUBLANES, hs), F32)], axis=0).T
        h2e_ref[rows, D_MODEL:ROW_W] = meta_t
    cnt_scr[...] = counts
    cnt_ref[...] = counts


def _const_spec(shape):
    zeros = (0,) * len(shape)
    return pl.BlockSpec(shape, lambda b, c: zeros, pipeline_mode=pl.Buffered(1))


def _mixer_tiles(bsz, seq):
    tq = min(SEQ_TILE // NB, seq)
    ts = tq * NB
    assert bsz % NB == 0 and seq % tq == 0 and tq % N_SLABS == 0 and ts >= POOL_HALO
    return ts, tq


def _mixer(x, g1, w_in, pool_bd, pool_scale, conv_w, conv_b, wr4, wi4, b_r, b_i, lam, pa, pb, wo, g2,
           wrt, brt, w_slab):
    bsz, seq, d = x.shape
    ts, tq = _mixer_tiles(bsz, seq)
    hs = ts // POST_SPLIT
    assert d == D_MODEL and hs % LANES == 0 and brt.shape == (RT_ROWS, hs)
    nc = seq // tq
    consts = (g1, w_in, pool_bd, pool_scale, conv_w, conv_b, wr4, wi4, b_r, b_i, lam, pa, pb, wo, g2,
              wrt, brt, w_slab)
    return pl.pallas_call(
        _mixer_kernel,
        grid=(bsz // NB, nc),
        in_specs=[pl.BlockSpec(memory_space=pl.ANY)] + [_const_spec(a.shape) for a in consts],
        out_specs=[
            pl.BlockSpec((ts, d), lambda g, c: (g * nc + c, 0)),
            pl.BlockSpec((ts, ROW_W), lambda g, c: (g * nc + c, 0)),
            pl.BlockSpec((RT_ROWS, hs), lambda g, c: (0, 0)),
            pl.BlockSpec((SUBLANES, ts), lambda g, c: (0, g * nc + c)),
        ],
        out_shape=[
            jax.ShapeDtypeStruct((bsz * seq, d), F32),
            jax.ShapeDtypeStruct((bsz * seq, ROW_W), F32),
            jax.ShapeDtypeStruct((RT_ROWS, hs), F32),
            jax.ShapeDtypeStruct((SUBLANES, bsz * seq), F32),
        ],
        scratch_shapes=[
            pltpu.VMEM((2, tq, NB, d), F32),
            pltpu.SemaphoreType.DMA((2,)),
            pltpu.VMEM((POOL_HALO + ts, POOL_WIDTH), F32),
            pltpu.VMEM((CONV_HALO + ts, LRU_WIDTH), F32),
            pltpu.VMEM((ts, LRU_WIDTH), F32),
            pltpu.VMEM((ts, LRU_WIDTH), F32),
            pltpu.VMEM((ts, D_MODEL), F32),
            pltpu.VMEM((ts, D_MODEL), BF16),
            pltpu.VMEM((N_SLABS, ts, MXU_DIM), F32),
            pltpu.VMEM((NB, LRU_WIDTH), F32),
            pltpu.VMEM((RT_ROWS, hs), F32),
        ],
        compiler_params=pltpu.CompilerParams(
            dimension_semantics=("arbitrary", "arbitrary"), vmem_limit_bytes=VMEM_LIMIT),
        name="mixer",
    )(x, *consts)


def _dispatch_kernel(pos_ref, ztile_ref, h2e_ref, xs_ref, zbuf, sem, zsem):
    tb = h2e_ref.shape[0]
    tm = zbuf.shape[0]
    base = pl.program_id(0) * tb

    @pl.when(pl.program_id(0) == 0)
    def _():
        zbuf[...] = jnp.zeros_like(zbuf)

        def zero_copy(c):
            start = pl.multiple_of(ztile_ref[c], SUBLANES)
            return pltpu.make_async_copy(zbuf, xs_ref.at[pl.ds(start, tm)], zsem)

        for c in range(N_CLASSES):
            @pl.when(ztile_ref[c] >= 0)
            def _():
                zero_copy(c).start()
        for c in range(N_CLASSES):
            @pl.when(ztile_ref[c] >= 0)
            def _():
                zero_copy(c).wait()

        def tail_copy(k):
            return pltpu.make_async_copy(zbuf, xs_ref.at[pl.ds(pl.multiple_of(k * tm, SUBLANES), tm)], zsem)

        n_used = ztile_ref[N_CLASSES]
        n_all = xs_ref.shape[0] // tm
        lax.fori_loop(n_used, n_all, lambda k, c: (tail_copy(k).start(), c)[1], 0)
        lax.fori_loop(n_used, n_all, lambda k, c: (tail_copy(k).wait(), c)[1], 0)

    for r in range(tb):
        pltpu.make_async_copy(
            h2e_ref.at[pl.ds(r, 1)], xs_ref.at[pl.ds(pos_ref[base + r], 1)], sem).start(priority=r % 2)
    pltpu.make_async_copy(h2e_ref, xs_ref.at[pl.ds(0, tb)], sem).wait()


def _dispatch(pos, ztile, h2e, n_slots, tm):
    t = h2e.shape[0]
    tb = min(DISPATCH_TILE, t)
    assert t % tb == 0 and n_slots >= tb
    return pl.pallas_call(
        _dispatch_kernel,
        grid_spec=pltpu.PrefetchScalarGridSpec(
            num_scalar_prefetch=2,
            grid=(t // tb,),
            in_specs=[pl.BlockSpec((tb, ROW_W), lambda i, pos, zt: (i, 0))],
            out_specs=pl.BlockSpec(memory_space=pl.ANY),
            scratch_shapes=[pltpu.VMEM((tm, ROW_W), F32),
                            pltpu.SemaphoreType.DMA(()), pltpu.SemaphoreType.DMA(())],
        ),
        out_shape=jax.ShapeDtypeStruct((n_slots, ROW_W), F32),
        compiler_params=pltpu.CompilerParams(
            dimension_semantics=("arbitrary",), vmem_limit_bytes=VMEM_LIMIT),
        name="dispatch",
    )(pos, ztile, h2e)


def _expert_kernel(meta_ref, xs_ref, wg_lo, wu_lo, wd_lo, wg_hi, wu_hi, wd_hi, ys_ref):
    i = pl.program_id(0)

    @pl.when(i < meta_ref[0])
    def _():
        x = xs_ref[:, 0:D_MODEL].astype(BF16)
        w_lo = xs_ref[:, D_MODEL:D_MODEL + 1]
        w_hi = xs_ref[:, D_MODEL + 1:D_MODEL + 2]

        def ffn(wg, wu, wd, w):
            g = _dot(x, wg[0].astype(BF16))
            u = _dot(x, wu[0].astype(BF16))
            act = (g * _sigmoid(g)) * u * w
            return _dot(act.astype(BF16), wd[0].astype(BF16))

        ys_ref[...] = ffn(wg_lo, wu_lo, wd_lo, w_lo) + ffn(wg_hi, wu_hi, wd_hi, w_hi)

    @pl.when(i >= meta_ref[0])
    def _():
        ys_ref[...] = jnp.zeros_like(ys_ref)


def _experts(meta, xs, e_gate, e_up, e_down, n_tiles_max):
    tm = xs.shape[0] // n_tiles_max

    def row_map(i, m):
        return (jnp.minimum(i, m[0] - 1), 0)

    def lo_map(i, m):
        return (m[1 + jnp.minimum(i, m[0] - 1)], 0, 0)

    def hi_map(i, m):
        return (m[1 + n_tiles_max + jnp.minimum(i, m[0] - 1)], 0, 0)

    up_spec = lambda mp: pl.BlockSpec((1, D_MODEL, D_EXPERT), mp)
    down_spec = lambda mp: pl.BlockSpec((1, D_EXPERT, D_MODEL), mp)
    return pl.pallas_call(
        _expert_kernel,
        grid_spec=pltpu.PrefetchScalarGridSpec(
            num_scalar_prefetch=1,
            grid=(n_tiles_max,),
            in_specs=[pl.BlockSpec((tm, ROW_W), row_map),
                      up_spec(lo_map), up_spec(lo_map), down_spec(lo_map),
                      up_spec(hi_map), up_spec(hi_map), down_spec(hi_map)],
            out_specs=pl.BlockSpec((tm, D_MODEL), lambda i, m: (i, 0)),
        ),
        out_shape=jax.ShapeDtypeStruct((xs.shape[0], D_MODEL), F32),
        compiler_params=pltpu.CompilerParams(
            dimension_semantics=("arbitrary",), vmem_limit_bytes=VMEM_LIMIT),
        name="experts",
    )(meta, xs, e_gate, e_up, e_down, e_gate, e_up, e_down)


def _combine_kernel(pos_ref, x1_ref, gf_ref, ys_ref, out_hbm, ybuf, sems, obuf, osems):
    tb = ybuf.shape[1]
    tq = obuf.shape[1]
    nc = out_hbm.shape[1] // tq
    i = pl.program_id(0)
    last = i + 1 == pl.num_programs(0)

    def out_copies(blk, slot):
        g_, c_ = blk // nc, blk % nc
        return [pltpu.make_async_copy(obuf.at[slot, :, bb, :],
                                      out_hbm.at[g_ * NB + bb, pl.ds(c_ * tq, tq), :],
                                      osems.at[slot]) for bb in range(NB)]

    def gather(blk, slot):
        base = blk * tb
        for r in range(tb):
            pltpu.make_async_copy(ys_ref.at[pl.ds(pos_ref[base + r], 1)],
                                  ybuf.at[slot, pl.ds(r, 1)], sems.at[slot]).start(priority=r % 2)

    def finish(blk, slot):
        pltpu.make_async_copy(ys_ref.at[pl.ds(0, tb)], ybuf.at[slot], sems.at[slot]).wait()

        @pl.when(i >= 1)
        def _():
            for cp in out_copies(blk - 2, slot):
                cp.wait()

        rows = slice(slot * tb, (slot + 1) * tb)
        obuf[slot] = _rms_norm(x1_ref[rows, :] + ybuf[slot], gf_ref[...]).reshape(tq, NB, D_MODEL)
        for cp in out_copies(blk, slot):
            cp.start()

        @pl.when(jnp.logical_not(last))
        def _():
            gather(blk + 2, slot)

    @pl.when(i == 0)
    def _():
        gather(0, 0)
        gather(1, 1)

    finish(2 * i, 0)
    finish(2 * i + 1, 1)

    @pl.when(last)
    def _():
        for slot in range(2):
            for cp in out_copies(2 * i + slot, slot):
                cp.wait()


def _combine(pos, x1, gf, ys, bsz, seq):
    t = x1.shape[0]
    tb, tq = _mixer_tiles(bsz, seq)
    assert t % (2 * tb) == 0
    return pl.pallas_call(
        _combine_kernel,
        grid_spec=pltpu.PrefetchScalarGridSpec(
            num_scalar_prefetch=1,
            grid=(t // (2 * tb),),
            in_specs=[pl.BlockSpec((2 * tb, D_MODEL), lambda i, pos: (i, 0)),
                      pl.BlockSpec((1, D_MODEL), lambda i, pos: (0, 0)),
                      pl.BlockSpec(memory_space=pl.ANY)],
            out_specs=pl.BlockSpec(memory_space=pl.ANY),
            scratch_shapes=[pltpu.VMEM((2, tb, D_MODEL), F32), pltpu.SemaphoreType.DMA((2,)),
                            pltpu.VMEM((2, tq, NB, D_MODEL), F32), pltpu.SemaphoreType.DMA((2,))],
        ),
        out_shape=jax.ShapeDtypeStruct((bsz, seq, D_MODEL), F32),
        compiler_params=pltpu.CompilerParams(
            dimension_semantics=("arbitrary",), vmem_limit_bytes=VMEM_LIMIT),
        name="combine",
    )(pos, x1, gf, ys)


def _block_diag(w, per_block):
    n, k, _ = w.shape
    nb = n // per_block
    w = w.reshape(nb, per_block, k, k)
    eye = jnp.eye(per_block, dtype=w.dtype)
    out = jnp.einsum('bpij,pq->bpiqj', w, eye)
    return out.reshape(nb, per_block * k, per_block * k)


def _pair_tables():
    lo, hi = [], []
    for g in range(N_GROUPS):
        for a in range(EXP_PER_GROUP):
            for b in range(a + 1, EXP_PER_GROUP):
                lo.append(g * EXP_PER_GROUP + a)
                hi.append(g * EXP_PER_GROUP + b)
    return jnp.array(lo, jnp.int32), jnp.array(hi, jnp.int32)


def kernel(x, norm1_g, w_in, pool_w, pool_scale, conv_w, conv_b, rg_w_r, rg_b_r, rg_w_i, rg_b_i, rg_lambda, proj_a, proj_b, w_out, norm2_g, router_group_w, router_group_b, router_expert_w, router_expert_b, exp_w_gate, exp_w_up, exp_w_down, norm_f_g):
    bsz, seq, d = x.shape
    t = bsz * seq
    assert w_in.shape[0] == 1, "single-layer block"
    l = 0
    row = lambda v: v.reshape(1, -1).astype(F32)
    heads_per_blk = MXU_DIM // (LRU_WIDTH // rg_w_r.shape[1])
    groups_per_blk = MXU_DIM // POOL_GROUP_DIM
    n_rt = N_GROUPS + N_EXP
    wrt = jnp.concatenate([router_group_w[l].T, router_expert_w[l].T,
                           jnp.zeros((RT_ROWS - n_rt, d), F32)], axis=0)
    brt = jnp.concatenate([router_group_b[l], router_expert_b[l], jnp.zeros((RT_ROWS - n_rt,), F32)])
    hs = _mixer_tiles(bsz, seq)[0] // POST_SPLIT
    brt = jnp.broadcast_to(brt[:, None].astype(F32), (RT_ROWS, hs))

    w_in_b = w_in[l].astype(BF16)
    w_slab = jnp.concatenate([w_in_b[:, O_GATE:O_GATE + LRU_WIDTH], w_in_b[:, O_GB:O_GB + D_MODEL]], axis=1)
    w_slab = w_slab.reshape(d, N_SLABS, MXU_DIM).transpose(1, 0, 2)

    x1, h2e, cnt, route = _mixer(
        x, row(norm1_g[l]), jnp.concatenate([w_in_b[:, 0:O_GATE], w_in_b[:, O_GA:O_GB]], axis=1),
        _block_diag(pool_w[l], groups_per_blk).astype(BF16),
        row(pool_scale[l]), conv_w[l].astype(F32), row(conv_b[l]),
        _block_diag(rg_w_r[l], heads_per_blk).astype(BF16), _block_diag(rg_w_i[l], heads_per_blk).astype(BF16),
        row(rg_b_r[l]), row(rg_b_i[l]), row(rg_lambda[l]),
        proj_a[l].astype(BF16), proj_b[l].astype(BF16), w_out[l].astype(BF16), row(norm2_g[l]),
        wrt.astype(BF16), brt, w_slab)

    tm = min(EXPERT_TILE, t)
    n_tiles_max = -(-(t + N_CLASSES * (tm - 1)) // tm)
    cls = route[2].astype(jnp.int32)
    rank = route[3].astype(jnp.int32)
    counts = cnt[0:N_CLASSES, 0].astype(jnp.int32)
    padded = ((counts + tm - 1) // tm) * tm
    ends = jnp.cumsum(padded)
    offs = ends - padded
    pos = rank + jnp.sum(jnp.where(cls[:, None] == jnp.arange(N_CLASSES)[None, :], offs[None, :], 0), axis=1)
    n_tiles = ends[-1] // tm
    tile_cls = jnp.minimum(
        jnp.sum((jnp.arange(n_tiles_max)[:, None] * tm >= ends[None, :]).astype(jnp.int32), axis=1),
        N_CLASSES - 1)
    pair_lo, pair_hi = _pair_tables()
    meta = jnp.concatenate([n_tiles[None], pair_lo[tile_cls], pair_hi[tile_cls]]).astype(jnp.int32)

    ztile = jnp.concatenate([jnp.where(padded > 0, ends - tm, -1), n_tiles[None]]).astype(jnp.int32)
    xs = _dispatch(pos, ztile, h2e, n_tiles_max * tm, tm)
    ys = _experts(meta, xs, exp_w_gate[l], exp_w_up[l], exp_w_down[l], n_tiles_max)
    return _combine(pos, x1, row(norm_f_g), ys, bsz, seq)
```
